```python
import math
import jax
import jax.numpy as jnp
from jax import lax
import numpy as np

D_MODEL = 1024
BATCH = 8
SEQ = 2048
DEPTH = 4

GRID_W = 64
CTX_LEN = 256
Q_BLOCK = 128
ROPE_THETA = 10000.0
EPS = 1e-6

DA_HEADS = 4
DA_HD = 64
DA_VD = 2 * DA_HD
ML_HEADS = 4
ML_QK = 64
ML_V = 128
ML_CHUNK = 64
GQ_HEADS = 8
GQ_KV = 2
GQ_HD = 64
DN_HEADS = 4
DN_HD = 128
DN_CHUNK = 64
DN_CONV = 3
DN_QKV = 3 * DN_HEADS * DN_HD
D_FF = -(-8 * D_MODEL // (3 * 256)) * 256

N_EVEN = (DEPTH + 1) // 2
N_ODD = DEPTH // 2
EVEN_WIDTHS = [DA_HEADS * 2 * DA_HD, DA_HEADS * 2 * DA_HD, DA_HEADS * DA_VD,
               ML_HEADS * ML_QK, ML_HEADS * ML_QK, ML_HEADS * ML_V, ML_HEADS * ML_V, 4 * ML_HEADS]
E_IN = sum(EVEN_WIDTHS)
MIX_EVEN = DA_HEADS * DA_VD + ML_HEADS * ML_V
ODD_WIDTHS = [GQ_HEADS * GQ_HD, GQ_KV * GQ_HD, GQ_KV * GQ_HD, DN_QKV, DN_HEADS * DN_HD, 4 * DN_HEADS]
O_IN = sum(ODD_WIDTHS)
MIX_ODD = GQ_HEADS * GQ_HD + DN_HEADS * DN_HD

kernel_name = 'hybrid_diffusion_backbone'


def rms_norm(x, g):
    xf = x.astype(jnp.float32)
    y = xf * lax.rsqrt(jnp.mean(xf * xf, axis=-1, keepdims=True) + EPS)
    return (y * g.astype(jnp.float32)).astype(x.dtype)


def l2norm(t):
    tf = t.astype(jnp.float32)
    return (tf * lax.rsqrt(jnp.sum(tf * tf, axis=-1, keepdims=True) + EPS)).astype(t.dtype)


def modulate(h, shift, scale):
    return h * (1 + scale) + shift


def split_cols(p, widths):
    return jnp.split(p, np.cumsum(widths)[:-1].tolist(), axis=-1)


def to_heads(t, h):
    b, l, w = t.shape
    return t.reshape(b, l, h, w // h).transpose(0, 2, 1, 3)


def from_heads(t):
    b, h, l, d = t.shape
    return t.transpose(0, 2, 1, 3).reshape(b, l, h * d)


def flip_seq(t):
    return jnp.flip(t, axis=2)


def to_chunks(t, nc, c):
    return jnp.moveaxis(t.reshape(t.shape[:2] + (nc, c) + t.shape[3:]), 2, 0)


def from_chunks(t, l):
    t = jnp.moveaxis(t, 0, 2)
    return t.reshape(t.shape[:2] + (l,) + t.shape[4:])


def axial_rope_tables(n_tok, head_dim):
    rows = n_tok // GRID_W
    quarter = head_dim // 4
    freqs = ROPE_THETA ** (-jnp.arange(quarter, dtype=jnp.float32) / quarter)
    row = jnp.repeat(jnp.arange(rows, dtype=jnp.float32), GRID_W)
    col = jnp.tile(jnp.arange(GRID_W, dtype=jnp.float32), rows)
    ang = jnp.concatenate([row[:, None] * freqs, col[:, None] * freqs], axis=-1)
    return jnp.cos(ang), jnp.sin(ang)


def apply_rope(x, cos, sin):
    shp = x.shape
    q = shp[-1] // 4
    xr = x.reshape(shp[:-1] + (2, 2, q))
    x1, x2 = xr[..., 0, :], xr[..., 1, :]
    c = cos.reshape(cos.shape[0], 2, q)
    s = sin.reshape(sin.shape[0], 2, q)
    out = jnp.stack([x1 * c - x2 * s, x2 * c + x1 * s], axis=-2)
    return out.reshape(shp).astype(x.dtype)


def softmax_mix(q, k, v, coef):
    scale = q.shape[-1] ** -0.5
    s = jnp.einsum('mbhgqd,mbhkd->mbhgqk', q, k).astype(jnp.float32) * scale
    p = jnp.einsum('m,mbhgqk->bhgqk', coef.astype(jnp.float32), jax.nn.softmax(s, axis=-1))
    return jnp.einsum('bhgqk,bhkv->bhgqv', p.astype(v.dtype), v)


def blocked_softmax_mix(q, k, v, coef):
    m, b, hk, g, lq, d = q.shape
    nb = lq // Q_BLOCK
    qb = jnp.moveaxis(q.reshape(m, b, hk, g, nb, Q_BLOCK, d), 4, 0)
    out = lax.map(lambda qi: softmax_mix(qi, k, v, coef), qb)
    return jnp.moveaxis(out, 0, 4).reshape(b, hk, g, lq, v.shape[-1])


def short_conv(x, w):
    kw = w.shape[0]
    pad = (kw - 1) // 2
    y = lax.conv_general_dilated(x, w[:, None, :].astype(x.dtype), window_strides=(1,),
                                 padding=[(pad, kw - 1 - pad)], dimension_numbers=('NWC', 'WIO', 'NWC'),
                                 feature_group_count=x.shape[-1])
    return jax.nn.silu(y)


def mlstm_scan(q, k, v, i_pre, f_pre, state, emit):
    b, h, l, dk = q.shape
    nc = l // ML_CHUNK
    f32 = jnp.float32
    logf = jax.nn.log_sigmoid(f_pre.astype(f32))
    xs = tuple(to_chunks(t, nc, ML_CHUNK) for t in (q.astype(f32), k.astype(f32), v.astype(f32), i_pre.astype(f32), logf))
    causal = jnp.tril(jnp.ones((ML_CHUNK, ML_CHUNK), bool))

    def step(carry, inp):
        c_st, n_st, m_st = carry
        qc, kc, vc, ic, lfc = inp
        bcum = jnp.cumsum(lfc, axis=-1)
        b_end = bcum[..., -1]
        w_end = b_end[..., None] - bcum + ic
        m_new = jnp.maximum(b_end + m_st, jnp.max(w_end, axis=-1))
        decay = jnp.exp(b_end + m_st - m_new)
        w = jnp.exp(w_end - m_new[..., None])
        c_new = decay[..., None, None] * c_st + jnp.einsum('bhs,bhsk,bhsv->bhkv', w, kc, vc)
        n_new = decay[..., None] * n_st + jnp.einsum('bhs,bhsk->bhk', w, kc)
        out = None
        if emit:
            a = bcum + m_st[..., None]
            dmat = jnp.where(causal, bcum[..., :, None] - bcum[..., None, :] + ic[..., None, :], -jnp.inf)
            mt = jnp.maximum(a, jnp.max(dmat, axis=-1))
            inter = jnp.exp(a - mt)
            p = jnp.exp(dmat - mt[..., None]) * jnp.einsum('bhtk,bhsk->bhts', qc, kc)
            num = inter[..., None] * jnp.einsum('bhtk,bhkv->bhtv', qc, c_st) + jnp.einsum('bhts,bhsv->bhtv', p, vc)
            den = inter * jnp.einsum('bhtk,bhk->bht', qc, n_st) + jnp.sum(p, axis=-1)
            out = num / jnp.maximum(jnp.abs(den), jnp.exp(-mt))[..., None]
        return (c_new, n_new, m_new), out

    final, hs = lax.scan(step, state, xs)
    hout = from_chunks(hs, l).astype(v.dtype) if emit else None
    return hout, final


def deltanet_scan(q, k, v, log_decay, beta, state, emit):
    b, h, l, dk = q.shape
    dv = v.shape[-1]
    nc = l // DN_CHUNK
    f32 = jnp.float32
    xs = tuple(to_chunks(t.astype(f32), nc, DN_CHUNK) for t in (q, k, v, log_decay, beta))
    incl = jnp.tril(jnp.ones((DN_CHUNK, DN_CHUNK), bool))
    strict = jnp.tril(jnp.ones((DN_CHUNK, DN_CHUNK), bool), -1)

    def step(s, inp):
        qc, kc, vc, gc, bc = inp
        gcum = jnp.cumsum(gc, axis=-1)
        decay_ts = jnp.exp(jnp.where(incl, gcum[..., :, None] - gcum[..., None, :], -jnp.inf))
        a = jnp.where(strict, bc[..., :, None] * decay_ts * jnp.einsum('bhtk,bhsk->bhts', kc, kc), 0.0)
        gam = jnp.exp(gcum)
        rhs = jnp.concatenate([bc[..., None] * vc, (bc * gam)[..., None] * kc], axis=-1)
        sol = lax.linalg.triangular_solve(a, rhs, left_side=True, lower=True, unit_diagonal=True)
        u = sol[..., :dv] - jnp.einsum('bhtk,bhkv->bhtv', sol[..., dv:], s)
        g_end = gcum[..., -1]
        s_new = jnp.exp(g_end)[..., None, None] * s + jnp.einsum(
            'bhsk,bhsv->bhkv', kc * jnp.exp(g_end[..., None] - gcum)[..., None], u)
        out = None
        if emit:
            attn = jnp.einsum('bhtk,bhsk->bhts', qc, kc) * decay_ts
            out = gam[..., None] * jnp.einsum('bhtk,bhkv->bhtv', qc, s) + jnp.einsum('bhts,bhsv->bhtv', attn, u)
        return s_new, out

    s_fin, outs = lax.scan(step, state, xs)
    o = from_chunks(outs, l).astype(v.dtype) if emit else None
    return o, s_fin


def diff_attention(q_l, k_l, v_l, q_c, k_c, v_c, cos, sin, lam, lam_init, norm_g, emit_ctx):
    def maps(t):
        b, l, _ = t.shape
        return t.reshape(b, l, DA_HEADS, 2, DA_HD).transpose(3, 0, 2, 1, 4)
    ql = apply_rope(maps(q_l), cos, sin)
    kl = apply_rope(maps(k_l), cos, sin)
    qc, kc = maps(q_c), maps(k_c)
    vl, vc = to_heads(v_l, DA_HEADS), to_heads(v_c, DA_HEADS)
    lf = lam.astype(jnp.float32)
    lmb = jnp.exp(jnp.sum(lf[0] * lf[1])) - jnp.exp(jnp.sum(lf[2] * lf[3])) + lam_init
    coef = jnp.stack([jnp.ones_like(lmb), -lmb])
    k_all = jnp.concatenate([kc, kl], axis=3)
    v_all = jnp.concatenate([vc, vl], axis=2)

    def finish(o):
        return from_heads(rms_norm(o[:, :, 0], norm_g) * (1.0 - lam_init))
    y_l = finish(blocked_softmax_mix(ql[:, :, :, None], k_all, v_all, coef))
    y_c = finish(softmax_mix(qc[:, :, :, None], kc, vc, coef)) if emit_ctx else None
    return y_l, y_c


def mlstm_mixer(q_l, k_l, v_l, o_l, g_l, q_c, k_c, v_c, o_c, g_c, gate_bias, norm_g, emit_ctx):
    def prep(q, k, v, g):
        b, l, _ = g.shape
        gates = (g + gate_bias).astype(jnp.float32).reshape(b, l, 4, ML_HEADS).transpose(2, 0, 3, 1)
        return to_heads(q, ML_HEADS), to_heads(k, ML_HEADS) * (ML_QK ** -0.5), to_heads(v, ML_HEADS), gates
    ql, kl, vl, gl = prep(q_l, k_l, v_l, g_l)
    qc, kc, vc, gc = prep(q_c, k_c, v_c, g_c)
    b = ql.shape[0]
    zero = (jnp.zeros((b, ML_HEADS, ML_QK, ML_V), jnp.float32),
            jnp.zeros((b, ML_HEADS, ML_QK), jnp.float32),
            jnp.zeros((b, ML_HEADS), jnp.float32))
    hc_f, st_f = mlstm_scan(qc, kc, vc, gc[0], gc[1], zero, emit_ctx)
    hl_f, _ = mlstm_scan(ql, kl, vl, gl[0], gl[1], st_f, True)
    rc = [flip_seq(t) for t in (qc, kc, vc, gc[2], gc[3])]
    rl = [flip_seq(t) for t in (ql, kl, vl, gl[2], gl[3])]
    hc_b, st_b = mlstm_scan(rc[0], rc[1], rc[2], rc[3], rc[4], zero, emit_ctx)
    hl_b, _ = mlstm_scan(rl[0], rl[1], rl[2], rl[3], rl[4], st_b, True)

    def finish(hsum, o):
        return from_heads(rms_norm(hsum, norm_g)) * jax.nn.sigmoid(o)
    y_l = finish(hl_f + flip_seq(hl_b), o_l)
    y_c = finish(hc_f + flip_seq(hc_b), o_c) if emit_ctx else None
    return y_l, y_c


def gqa_attention(q_l, k_l, v_l, q_c, k_c, v_c, cos, sin, qk_g, emit_ctx):
    def prep(q, k, v):
        return (rms_norm(to_heads(q, GQ_HEADS), qk_g[0]), rms_norm(to_heads(k, GQ_KV), qk_g[1]), to_heads(v, GQ_KV))
    ql, kl, vl = prep(q_l, k_l, v_l)
    qc, kc, vc = prep(q_c, k_c, v_c)
    ql, kl = apply_rope(ql, cos, sin), apply_rope(kl, cos, sin)

    def group(q):
        b, h, l, d = q.shape
        return q.reshape(b, GQ_KV, h // GQ_KV, l, d)[None]

    def finish(o):
        b, hk, g, l, d = o.shape
        return from_heads(o.reshape(b, hk * g, l, d))
    coef = jnp.ones((1,), jnp.float32)
    k_all = jnp.concatenate([kc, kl], axis=2)[None]
    v_all = jnp.concatenate([vc, vl], axis=2)
    y_l = finish(blocked_softmax_mix(group(ql), k_all, v_all, coef))
    y_c = finish(softmax_mix(group(qc), kc[None], vc, coef)) if emit_ctx else None
    return y_l, y_c


def gated_deltanet(qkv_l, z_l, g_l, qkv_c, z_c, g_c, conv_w, a_log, dt_bias, norm_g, emit_ctx):
    def prep(qkv, g):
        q, k, v = jnp.split(short_conv(qkv, conv_w), 3, axis=-1)
        b, l, _ = g.shape
        gates = g.astype(jnp.float32).reshape(b, l, 2, 2, DN_HEADS)
        beta = jax.nn.sigmoid(gates[:, :, :, 0]).transpose(2, 0, 3, 1)
        log_decay = (-jnp.exp(a_log.astype(jnp.float32))
                     * jax.nn.softplus(gates[:, :, :, 1] + dt_bias.astype(jnp.float32))).transpose(2, 0, 3, 1)
        return (l2norm(to_heads(q, DN_HEADS)) * (DN_HD ** -0.5), l2norm(to_heads(k, DN_HEADS)),
                to_heads(v, DN_HEADS), beta, log_decay)
    ql, kl, vl, bl, dl = prep(qkv_l, g_l)
    qc, kc, vc, bc, dc = prep(qkv_c, g_c)
    zero = jnp.zeros((ql.shape[0], DN_HEADS, DN_HD, DN_HD), jnp.float32)
    oc_f, s_f = deltanet_scan(qc, kc, vc, dc[0], bc[0], zero, emit_ctx)
    ol_f, _ = deltanet_scan(ql, kl, vl, dl[0], bl[0], s_f, True)
    rc = [flip_seq(t) for t in (qc, kc, vc, dc[1], bc[1])]
    rl = [flip_seq(t) for t in (ql, kl, vl, dl[1], bl[1])]
    oc_b, s_b = deltanet_scan(rc[0], rc[1], rc[2], rc[3], rc[4], zero, emit_ctx)
    ol_b, _ = deltanet_scan(rl[0], rl[1], rl[2], rl[3], rl[4], s_b, True)

    def finish(o, z):
        return from_heads(rms_norm(o, norm_g)) * jax.nn.silu(z)
    y_l = finish(ol_f + flip_seq(ol_b), z_l)
    y_c = finish(oc_f + flip_seq(oc_b), z_c) if emit_ctx else None
    return y_l, y_c


def mix_even(pl, pc, cos, sin, lam, lam_init, gate_bias, da_g, ml_g, emit_ctx):
    aq_l, ak_l, av_l, bq_l, bk_l, bv_l, bo_l, bg_l = split_cols(pl, EVEN_WIDTHS)
    aq_c, ak_c, av_c, bq_c, bk_c, bv_c, bo_c, bg_c = split_cols(pc, EVEN_WIDTHS)
    ya_l, ya_c = diff_attention(aq_l, ak_l, av_l, aq_c, ak_c, av_c, cos, sin, lam, lam_init, da_g, emit_ctx)
    yb_l, yb_c = mlstm_mixer(bq_l, bk_l, bv_l, bo_l, bg_l, bq_c, bk_c, bv_c, bo_c, bg_c, gate_bias, ml_g, emit_ctx)
    y_l = jnp.concatenate([ya_l, yb_l], axis=-1)
    y_c = jnp.concatenate([ya_c, yb_c], axis=-1) if emit_ctx else None
    return y_l, y_c


def mix_odd(pl, pc, cos, sin, qk_g, conv_w, a_log, dt_bias, dn_g, emit_ctx):
    cq_l, ck_l, cv_l, dqkv_l, dz_l, dg_l = split_cols(pl, ODD_WIDTHS)
    cq_c, ck_c, cv_c, dqkv_c, dz_c, dg_c = split_cols(pc, ODD_WIDTHS)
    yc_l, yc_c = gqa_attention(cq_l, ck_l, cv_l, cq_c, ck_c, cv_c, cos, sin, qk_g, emit_ctx)
    yd_l, yd_c = gated_deltanet(dqkv_l, dz_l, dg_l, dqkv_c, dz_c, dg_c, conv_w, a_log, dt_bias, dn_g, emit_ctx)
    y_l = jnp.concatenate([yc_l, yd_l], axis=-1)
    y_c = jnp.concatenate([yc_c, yd_c], axis=-1) if emit_ctx else None
    return y_l, y_c


def swiglu(h, wg, wu, wd):
    return (jax.nn.silu(h @ wg) * (h @ wu)) @ wd


def setup_inputs(seed: int = 0) -> dict:
    key = jax.random.key(seed)
    ks = iter(jax.random.split(key, 32))
    f32 = jnp.float32

    def nrm(shape, scale):
        return jax.random.normal(next(ks), shape, f32) * scale
    d = D_MODEL
    x = nrm((BATCH, SEQ, d), 1.0)
    c = nrm((BATCH, d), 1.0)
    ctx = nrm((BATCH, CTX_LEN, d), 1.0)
    c_ctx = nrm((d,), 1.0)
    w_ada = nrm((DEPTH, d, 6 * d), 0.5 * d ** -0.5)
    b_ada = nrm((DEPTH, 6 * d), 0.02)
    g_mix = 1.0 + nrm((DEPTH, d), 0.02)
    g_ffn = 1.0 + nrm((DEPTH, d), 0.02)
    w_in_e = nrm((N_EVEN, d, E_IN), d ** -0.5)
    i_bias = nrm((N_EVEN, 2, ML_HEADS), 0.1)
    f_bias = jnp.linspace(3.0, 6.0, ML_HEADS, dtype=f32) + nrm((N_EVEN, 2, ML_HEADS), 0.1)
    b_gate_e = jnp.stack([i_bias, f_bias], axis=2).reshape(N_EVEN, 4 * ML_HEADS)
    da_lam = nrm((N_EVEN, 4, DA_HD), 0.1)
    da_norm_g = 1.0 + nrm((N_EVEN, DA_VD), 0.02)
    ml_norm_g = 1.0 + nrm((N_EVEN, ML_V), 0.02)
    w_out_e = nrm((N_EVEN, MIX_EVEN, d), MIX_EVEN ** -0.5)
    w_in_o = nrm((N_ODD, d, O_IN), d ** -0.5)
    qk_norm_g = 1.0 + nrm((N_ODD, 2, GQ_HD), 0.02)
    dn_conv = nrm((N_ODD, DN_CONV, DN_QKV), DN_CONV ** -0.5)
    dn_a_log = jnp.log(jnp.linspace(1.0, 16.0, DN_HEADS, dtype=f32)) + nrm((N_ODD, 2, DN_HEADS), 0.05)
    dt = jnp.exp(jax.random.uniform(next(ks), (N_ODD, 2, DN_HEADS), f32, math.log(1e-3), math.log(1e-1)))
    dn_dt_bias = dt + jnp.log(-jnp.expm1(-dt))
    dn_norm_g = 1.0 + nrm((N_ODD, DN_HD), 0.02)
    w_out_o = nrm((N_ODD, MIX_ODD, d), MIX_ODD ** -0.5)
    w_gate = nrm((DEPTH, d, D_FF), d ** -0.5)
    w_up = nrm((DEPTH, d, D_FF), d ** -0.5)
    w_down = nrm((DEPTH, D_FF, d), D_FF ** -0.5)
    g_final = 1.0 + nrm((d,), 0.02)
    return {'x': x, 'c': c, 'ctx': ctx, 'c_ctx': c_ctx, 'w_ada': w_ada, 'b_ada': b_ada,
            'g_mix': g_mix, 'g_ffn': g_ffn, 'w_in_e': w_in_e, 'b_gate_e': b_gate_e, 'da_lam': da_lam,
            'da_norm_g': da_norm_g, 'ml_norm_g': ml_norm_g, 'w_out_e': w_out_e, 'w_in_o': w_in_o,
            'qk_norm_g': qk_norm_g, 'dn_conv': dn_conv, 'dn_a_log': dn_a_log, 'dn_dt_bias': dn_dt_bias,
            'dn_norm_g': dn_norm_g, 'w_out_o': w_out_o, 'w_gate': w_gate, 'w_up': w_up, 'w_down': w_down,
            'g_final': g_final}


def reference(x, c, ctx, c_ctx, w_ada, b_ada, g_mix, g_ffn, w_in_e, b_gate_e, da_lam, da_norm_g, ml_norm_g,
              w_out_e, w_in_o, qk_norm_g, dn_conv, dn_a_log, dn_dt_bias, dn_norm_g, w_out_o, w_gate, w_up,
              w_down, g_final):
    n_tok = x.shape[1]
    cos, sin = axial_rope_tables(n_tok, DA_HD)
    silu_c = jax.nn.silu(c)
    silu_cc = jax.nn.silu(c_ctx)
    xl, xc = x, ctx
    for layer in range(DEPTH):
        emit_ctx = layer < DEPTH - 1
        mod_l = (silu_c @ w_ada[layer] + b_ada[layer])[:, None, :]
        mod_c = silu_cc @ w_ada[layer] + b_ada[layer]
        sh1, sc1, ga1, sh2, sc2, ga2 = jnp.split(mod_l, 6, axis=-1)
        sh1c, sc1c, ga1c, sh2c, sc2c, ga2c = jnp.split(mod_c, 6, axis=-1)
        hl = modulate(rms_norm(xl, g_mix[layer]), sh1, sc1)
        hc = modulate(rms_norm(xc, g_mix[layer]), sh1c, sc1c)
        if layer % 2 == 0:
            e = layer // 2
            lam_init = 0.8 - 0.6 * math.exp(-0.3 * layer)
            yl, yc = mix_even(hl @ w_in_e[e], hc @ w_in_e[e], cos, sin, da_lam[e], lam_init, b_gate_e[e],
                              da_norm_g[e], ml_norm_g[e], emit_ctx)
            w_out = w_out_e[e]
        else:
            o = layer // 2
            yl, yc = mix_odd(hl @ w_in_o[o], hc @ w_in_o[o], cos, sin, qk_norm_g[o], dn_conv[o], dn_a_log[o],
                             dn_dt_bias[o], dn_norm_g[o], emit_ctx)
            w_out = w_out_o[o]
        xl = xl + ga1 * (yl @ w_out)
        xl = xl + ga2 * swiglu(modulate(rms_norm(xl, g_ffn[layer]), sh2, sc2), w_gate[layer], w_up[layer], w_down[layer])
        if emit_ctx:
            xc = xc + ga1c * (yc @ w_out)
            xc = xc + ga2c * swiglu(modulate(rms_norm(xc, g_ffn[layer]), sh2c, sc2c), w_gate[layer], w_up[layer], w_down[layer])
    return rms_norm(xl, g_final)
```

```python
import functools
import math

import jax
import jax.numpy as jnp
import numpy as np
from jax import lax
from jax.experimental import pallas as pl
from jax.experimental.pallas import tpu as pltpu

F32 = jnp.float32
BF16 = jnp.bfloat16

D_MODEL = 1024
DEPTH = 4
GRID_W = 64
ROPE_THETA = 10000.0
EPS = 1e-6
HD = 64
Q_BLOCK = 128
LANES = 128
DA_HEADS = 4
ML_HEADS = 4
GQ_HEADS = 8
GQ_KV = 2
DN_HEADS = 4
D_FF = 2816
ML_CHUNK = 128
DN_CHUNK = 64
VMEM_LIMIT = 56 * 1024 * 1024

EVEN_BF = 1536
EVEN_F32 = 1792
ODD_BF = 1408
ODD_F32 = 2560


def _bf(x):
    return x.astype(BF16)


def _dot(a, b):
    return jnp.dot(a, b, preferred_element_type=F32)


def _dot_nt(a, b):
    return lax.dot_general(a, b, (((1,), (1,)), ((), ())), preferred_element_type=F32)


def _dot_tn(a, b):
    return lax.dot_general(a, b, (((0,), (0,)), ((), ())), preferred_element_type=F32)


def _split2(x):
    hi = _bf(x)
    return hi, _bf(x - hi.astype(F32))


def _split3(x):
    hi = _bf(x)
    r = x - hi.astype(F32)
    mid = _bf(r)
    return hi, mid, _bf(r - mid.astype(F32))


def _dot_x3(a, b):
    ah, al = _split2(a)
    bh, bl = _split2(b)
    return _dot(ah, bh) + (_dot(ah, bl) + _dot(al, bh))


def _sel_dot(sel, x):
    h, m, l = _split3(x)
    return _dot(sel, h) + (_dot(sel, m) + _dot(sel, l))


def _sel_dot_nt(sel, x):
    h, m, l = _split3(x)
    return _dot_nt(sel, h) + (_dot_nt(sel, m) + _dot_nt(sel, l))


def _sigmoid(x):
    return 1.0 / (1.0 + jnp.exp(-x))


def _silu(x):
    return x * _sigmoid(x)


def _log_sigmoid(x):
    return jnp.minimum(x, 0.0) - jnp.log(1.0 + jnp.exp(-jnp.abs(x)))


def _softplus(x):
    return jnp.maximum(x, 0.0) + jnp.log(1.0 + jnp.exp(-jnp.abs(x)))


def _rms(x, g, n=None):
    n = x.shape[-1] if n is None else n
    ss = jnp.sum(x * x, axis=-1, keepdims=True)
    return x * lax.rsqrt(ss * (1.0 / n) + EPS) * g


def _norm_mod(x, g, mod, slot):
    sh = mod[:, slot * D_MODEL:(slot + 1) * D_MODEL]
    sc = mod[:, (slot + 1) * D_MODEL:(slot + 2) * D_MODEL]
    return _rms(x, g) * (1.0 + sc) + sh


def _rope(x, c, s1, s2):
    return x * c + pltpu.roll(x, LANES - 16, 1) * s1 + pltpu.roll(x, 16, 1) * s2


def _params(*sem):
    return pltpu.CompilerParams(dimension_semantics=sem, vmem_limit_bytes=VMEM_LIMIT)


def _ada_kernel(cv_ref, w_ref, b_ref, o_ref):
    cv = cv_ref[...]
    o_ref[...] = _dot_x3(_silu(cv), w_ref[...]) + b_ref[...]


def _ada_mods(cv, w_ada, b_ada):
    rows = cv.shape[0]
    nb = 6
    return pl.pallas_call(
        _ada_kernel,
        grid=(DEPTH, nb),
        in_specs=[pl.BlockSpec((rows, D_MODEL), lambda l, j: (0, 0)),
                  pl.BlockSpec((None, D_MODEL, D_MODEL), lambda l, j: (l, 0, j)),
                  pl.BlockSpec((None, 1, D_MODEL), lambda l, j: (l, 0, j))],
        out_specs=pl.BlockSpec((None, rows, D_MODEL), lambda l, j: (l, 0, j)),
        out_shape=jax.ShapeDtypeStruct((DEPTH, rows, 6 * D_MODEL), F32),
        compiler_params=_params("arbitrary", "arbitrary"),
        name="ada_mods",
    )(cv, w_ada, b_ada.reshape(DEPTH, 1, 6 * D_MODEL))


def _inproj_even_kernel(*refs, rope):
    if rope:
        x_ref, mod_ref, g_ref, w_ref, rc_ref, s1_ref, s2_ref, o1_ref, o2_ref = refs
    else:
        x_ref, mod_ref, g_ref, w_ref, o1_ref, o2_ref = refs
    hb = _bf(_norm_mod(x_ref[...], g_ref[...], mod_ref[...], 0))
    for seg in range(4):
        p = _dot(hb, w_ref[:, seg * 256:(seg + 1) * 256])
        for half in range(2):
            gi = seg * 2 + half
            xg = p[:, half * LANES:(half + 1) * LANES]
            if rope:
                xg = _rope(xg, rc_ref[...], s1_ref[...], s2_ref[...])
            if gi < DA_HEADS:
                xg = xg * (HD ** -0.5)
            o1_ref[:, gi * LANES:(gi + 1) * LANES] = _bf(xg)
    o1_ref[:, 1024:EVEN_BF] = _bf(_dot(hb, w_ref[:, 1024:EVEN_BF]))
    o2_ref[...] = _dot(hb, w_ref[:, EVEN_BF:])


def _inproj_odd_kernel(*refs, rope):
    if rope:
        x_ref, mod_ref, g_ref, w_ref, qkg_ref, rc_ref, s1_ref, s2_ref, o1_ref, o2_ref = refs
    else:
        x_ref, mod_ref, g_ref, w_ref, qkg_ref, o1_ref, o2_ref = refs
    hb = _bf(_norm_mod(x_ref[...], g_ref[...], mod_ref[...], 0))
    for seg in range(5):
        p = _dot(hb, w_ref[:, seg * 256:(seg + 1) * 256])
        for half in range(2):
            gi = seg * 2 + half
            xg = p[:, half * LANES:(half + 1) * LANES]
            if gi < GQ_HEADS:
                grow = gi // (GQ_HEADS // GQ_KV)
            else:
                grow = GQ_KV + (gi - GQ_HEADS)
            xg = _rms(xg, qkg_ref[grow:grow + 1, :], HD)
            if rope:
                xg = _rope(xg, rc_ref[...], s1_ref[...], s2_ref[...])
            if gi < GQ_HEADS:
                xg = xg * (HD ** -0.5)
            o1_ref[:, gi * LANES:(gi + 1) * LANES] = _bf(xg)
    o1_ref[:, 1280:ODD_BF] = _bf(_dot(hb, w_ref[:, 1280:ODD_BF]))
    o2_ref[...] = _dot(hb, w_ref[:, ODD_BF:])


def _inproj(x, mods, g, w, extra, tables, *, even, seq_len, mod_row, tm):
    b, t, _ = x.shape
    rows = b * t
    nb = rows // tm
    bpb = max(t // tm, 1)
    n_bf, n_f32 = (EVEN_BF, EVEN_F32) if even else (ODD_BF, ODD_F32)
    rope = tables is not None
    kern = functools.partial(_inproj_even_kernel if even else _inproj_odd_kernel, rope=rope)
    in_specs = [pl.BlockSpec((tm, D_MODEL), lambda i: (i, 0)),
                pl.BlockSpec((None, 1, 6 * D_MODEL), lambda i: (mod_row(i), 0, 0)),
                pl.BlockSpec((1, D_MODEL), lambda i: (0, 0)),
                pl.BlockSpec((D_MODEL, n_bf + n_f32), lambda i: (0, 0))]
    args = [x.reshape(rows, D_MODEL), mods, g.reshape(1, D_MODEL), w]
    for e in extra:
        in_specs.append(pl.BlockSpec(e.shape, lambda i: (0, 0)))
        args.append(e)
    if rope:
        for tab in tables:
            in_specs.append(pl.BlockSpec((tm, LANES), lambda i: (i % bpb, 0)))
            args.append(tab)
    o1, o2 = pl.pallas_call(
        kern,
        grid=(nb,),
        in_specs=in_specs,
        out_specs=[pl.BlockSpec((tm, n_bf), lambda i: (i, 0)),
                   pl.BlockSpec((tm, n_f32), lambda i: (i, 0))],
        out_shape=[jax.ShapeDtypeStruct((rows, n_bf), BF16),
                   jax.ShapeDtypeStruct((rows, n_f32), F32)],
        compiler_params=_params("arbitrary"),
        name="inproj_even" if even else "inproj_odd",
    )(*args)
    return o1.reshape(b, t, n_bf), o2.reshape(b, t, n_f32)


def _softmax_pv(q, kvs):
    ss = [_dot_nt(q, k) for k, _ in kvs]
    mx = ss[0].max(axis=-1, keepdims=True)
    for s in ss[1:]:
        mx = jnp.maximum(mx, s.max(axis=-1, keepdims=True))
    den = None
    acc = None
    for s, (_, v) in zip(ss, kvs):
        p = jnp.exp(s - mx)
        l = p.sum(axis=-1, keepdims=True)
        o = _dot(_bf(p), v)
        den = l if den is None else den + l
        acc = o if acc is None else acc + o
    return acc / den


def _store_heads(o_ref, o, group, width, blocked):
    if not blocked:
        o_ref[:, group * LANES:(group + 1) * LANES] = o
        return
    for jj in range(o.shape[0] // Q_BLOCK):
        col = jj * width + group * LANES
        o_ref[:, col:col + LANES] = o[jj * Q_BLOCK:(jj + 1) * Q_BLOCK, :]


def _attn_out(b, t, width, tq, blocked):
    if blocked:
        nblk = t // Q_BLOCK
        spec = pl.BlockSpec((None, Q_BLOCK, (tq // Q_BLOCK) * width), lambda bi, qi: (bi, 0, qi))
        return spec, jax.ShapeDtypeStruct((b, Q_BLOCK, nblk * width), F32)
    return (pl.BlockSpec((None, tq, width), lambda bi, qi: (bi, qi, 0)),
            jax.ShapeDtypeStruct((b, t, width), F32))


def _diff_attn_kernel(*refs, n_kv, lam_init, blocked):
    q_ref = refs[0]
    kv_refs = refs[1:1 + 2 * n_kv]
    lam_ref, ng_ref, o_ref = refs[1 + 2 * n_kv:]
    lane = lax.broadcasted_iota(jnp.int32, (1, LANES), 1)
    lf = lam_ref[...]
    s1 = jnp.sum(lf[0:1, :] * lf[1:2, :], axis=-1, keepdims=True)
    s2 = jnp.sum(lf[2:3, :] * lf[3:4, :], axis=-1, keepdims=True)
    lmb = jnp.exp(s1) - jnp.exp(s2) + lam_init
    ng = ng_ref[...]
    for h in range(DA_HEADS):
        hs = slice(h * LANES, (h + 1) * LANES)
        q = q_ref[:, hs]
        kvs = [(kv_refs[2 * i][:, hs], kv_refs[2 * i + 1][:, hs]) for i in range(n_kv)]
        outs = []
        for m in range(2):
            in_map = (lane >= m * HD) & (lane < (m + 1) * HD)
            qm = jnp.where(in_map, q, jnp.zeros_like(q))
            outs.append(_softmax_pv(qm, kvs))
        o = outs[0] - lmb * outs[1]
        _store_heads(o_ref, _rms(o, ng) * (1.0 - lam_init), h, DA_HEADS * LANES, blocked)


def _diff_attn(q_src, kv_srcs, lam, ng, lam_init, tq, blocked):
    b, t, _ = q_src.shape
    nq = t // tq
    width = DA_HEADS * LANES
    in_specs = [pl.BlockSpec((None, tq, width), lambda bi, qi: (bi, qi, 0))]
    args = [q_src]
    for src in kv_srcs:
        n = src.shape[1]
        in_specs.append(pl.BlockSpec((None, n, width), lambda bi, qi: (bi, 0, 1)))
        in_specs.append(pl.BlockSpec((None, n, width), lambda bi, qi: (bi, 0, 2)))
        args += [src, src]
    in_specs += [pl.BlockSpec((4, HD), lambda bi, qi: (0, 0)),
                 pl.BlockSpec((1, LANES), lambda bi, qi: (0, 0))]
    args += [lam, ng.reshape(1, LANES)]
    out_spec, out_shape = _attn_out(b, t, width, tq, blocked)
    out = pl.pallas_call(
        functools.partial(_diff_attn_kernel, n_kv=len(kv_srcs), lam_init=lam_init, blocked=blocked),
        grid=(b, nq),
        in_specs=in_specs,
        out_specs=out_spec,
        out_shape=out_shape,
        compiler_params=_params("arbitrary", "arbitrary"),
        name="diff_attn",
    )(*args)
    return out.reshape(b, t, width)


def _gqa_kernel(*refs, n_kv, blocked):
    q_ref = refs[0]
    kv_refs = refs[1:1 + 2 * n_kv]
    o_ref = refs[1 + 2 * n_kv]
    lane = lax.broadcasted_iota(jnp.int32, (1, LANES), 1)
    low = lane < HD
    per_kv = GQ_HEADS // GQ_KV
    for pair in range(GQ_HEADS // 2):
        j = (2 * pair) // per_kv
        kvs = [(kv_refs[2 * i][:, j * LANES:(j + 1) * LANES], kv_refs[2 * i + 1][...]) for i in range(n_kv)]
        o_a = _softmax_pv(q_ref[:, (2 * pair) * LANES:(2 * pair + 1) * LANES], kvs)
        o_b = _softmax_pv(q_ref[:, (2 * pair + 1) * LANES:(2 * pair + 2) * LANES], kvs)
        if j == 0:
            packed = jnp.where(low, o_a, pltpu.roll(o_b, HD, 1))
        else:
            packed = jnp.where(low, pltpu.roll(o_a, HD, 1), o_b)
        _store_heads(o_ref, packed, pair, GQ_HEADS * HD, blocked)


def _gqa_attn(q_src, kv_srcs, tq, blocked):
    b, t, _ = q_src.shape
    nq = t // tq
    qw = GQ_HEADS * LANES
    in_specs = [pl.BlockSpec((None, tq, qw), lambda bi, qi: (bi, qi, 0))]
    args = [q_src]
    for src in kv_srcs:
        n = src.shape[1]
        in_specs.append(pl.BlockSpec((None, n, GQ_KV * LANES), lambda bi, qi: (bi, 0, qw // (GQ_KV * LANES))))
        in_specs.append(pl.BlockSpec((None, n, LANES), lambda bi, qi: (bi, 0, (qw + GQ_KV * LANES) // LANES)))
        args += [src, src]
    out_spec, out_shape = _attn_out(b, t, GQ_HEADS * HD, tq, blocked)
    out = pl.pallas_call(
        functools.partial(_gqa_kernel, n_kv=len(kv_srcs), blocked=blocked),
        grid=(b, nq),
        in_specs=in_specs,
        out_specs=out_spec,
        out_shape=out_shape,
        compiler_params=_params("arbitrary", "arbitrary"),
        name="gqa_attn",
    )(*args)
    return out.reshape(b, t, GQ_HEADS * HD)


def _mlstm_kernel(ql_ref, kl_ref, vl_ref, ol_ref, gl_ref, qc_ref, kc_ref, vc_ref, oc_ref, gc_ref,
                  gb_ref, ng_ref, yl_ref, yc_ref, hbl_ref, hbc_ref, cst_ref, mst_ref, *, n_lat, n_ctx):
    c = ML_CHUNK
    lane = lax.broadcasted_iota(jnp.int32, (1, LANES), 1)
    ri = lax.broadcasted_iota(jnp.int32, (c, c), 0)
    ci = lax.broadcasted_iota(jnp.int32, (c, c), 1)
    masks = (ci <= ri, ci >= ri)
    cums = tuple(_bf(m.astype(F32)) for m in masks)
    ones_col = jnp.broadcast_to((lane == 0).astype(F32), (c, LANES))
    gb = gb_ref[...]

    cst_ref[...] = jnp.zeros_like(cst_ref)
    mst_ref[...] = jnp.zeros_like(mst_ref)

    def chunk_step(q_ref, k_ref, v_ref, g_ref, dst_refs, rows):
        for d in range(2):
            r0 = rows[d]
            g = g_ref[pl.ds(r0, c), :] + gb
            lf = _log_sigmoid(g)
            cum = _sel_dot(cums[d], lf)
            q2 = q_ref[pl.ds(r0, c), :]
            k2 = k_ref[pl.ds(r0, c), :] * (HD ** -0.5)
            v2 = v_ref[pl.ds(r0, c), :]
            for hh in range(2):
                idx = d * 2 + hh
                c_i = d * 4 + hh
                c_f = d * 4 + 2 + hh
                i_col = g[:, c_i:c_i + 1]
                b_col = cum[:, c_f:c_f + 1]
                b_end = jnp.sum(lf[:, c_f:c_f + 1], axis=0, keepdims=True)
                sel_i = jnp.broadcast_to(_bf((lane == c_i).astype(F32)), (c, LANES))
                sel_f = jnp.broadcast_to(_bf((lane == c_f).astype(F32)), (c, LANES))
                i_row = _sel_dot_nt(sel_i, g)
                b_row = _sel_dot_nt(sel_f, cum)
                m_st = mst_ref[idx][0:1, 0:1]
                c_st = cst_ref[idx]
                in_head = (lane >= hh * HD) & (lane < (hh + 1) * HD)
                qm = _bf(jnp.where(in_head, q2, 0.0))
                km = _bf(jnp.where(in_head, k2, 0.0))
                vaug = jnp.concatenate([v2[:, hh * LANES:(hh + 1) * LANES], ones_col], axis=1)

                w_end = b_end - b_col + i_col
                m_new = jnp.maximum(b_end + m_st, jnp.max(w_end, axis=0, keepdims=True))
                decay = jnp.exp(b_end + m_st - m_new)
                w = jnp.exp(w_end - m_new)

                a = b_col + m_st
                dm = jnp.where(masks[d], b_col - b_row + i_row, -jnp.inf)
                mt = jnp.maximum(a, jnp.max(dm, axis=-1, keepdims=True))
                inter = jnp.exp(a - mt)
                p = jnp.exp(dm - mt) * _dot_nt(qm, km)
                nd = inter * _dot(qm, _bf(c_st)) + _dot(_bf(p), _bf(vaug))
                den = jnp.maximum(jnp.abs(nd[:, LANES:LANES + 1]), jnp.exp(-mt))
                dst_refs[d][pl.ds(r0, c), hh * LANES:(hh + 1) * LANES] = nd[:, :LANES] / den

                cst_ref[idx] = decay * c_st + _dot_tn(km, _bf(w * vaug))
                mst_ref[idx] = jnp.broadcast_to(m_new, mst_ref.shape[1:])

    for cc in range(n_ctx):
        chunk_step(qc_ref, kc_ref, vc_ref, gc_ref, (yc_ref, hbc_ref), (cc * c, (n_ctx - 1 - cc) * c))

    def body(cc, carry):
        rows = (pl.multiple_of(cc * c, c), pl.multiple_of((n_lat - 1 - cc) * c, c))
        chunk_step(ql_ref, kl_ref, vl_ref, gl_ref, (yl_ref, hbl_ref), rows)
        return carry

    lax.fori_loop(0, n_lat, body, 0)

    ng = ng_ref[...]

    def finish(y_ref, hb_ref, o_ref, n_rows):
        tile = 256
        for r in range(0, n_rows, tile):
            for hh in range(2):
                sl = (slice(r, r + tile), slice(hh * LANES, (hh + 1) * LANES))
                hsum = y_ref[sl] + hb_ref[sl]
                y_ref[sl] = _rms(hsum, ng) * _sigmoid(o_ref[sl])

    finish(yc_ref, hbc_ref, oc_ref, n_ctx * c)
    finish(yl_ref, hbl_ref, ol_ref, n_lat * c)


def _mlstm(pl_f32, pc_f32, gate_bias, ng):
    b, t, _ = pl_f32.shape
    tc = pc_f32.shape[1]
    n_pairs = ML_HEADS // 2
    def specs(n):
        return [pl.BlockSpec((None, n, LANES), lambda bi, p: (bi, 0, p)),
                pl.BlockSpec((None, n, LANES), lambda bi, p: (bi, 0, 2 + p)),
                pl.BlockSpec((None, n, 2 * LANES), lambda bi, p: (bi, 0, 2 + p)),
                pl.BlockSpec((None, n, 2 * LANES), lambda bi, p: (bi, 0, 4 + p)),
                pl.BlockSpec((None, n, LANES), lambda bi, p: (bi, 0, 12 + p))]
    in_specs = specs(t) + specs(tc) + [pl.BlockSpec((None, 1, LANES), lambda bi, p: (p, 0, 0)),
                                      pl.BlockSpec((1, LANES), lambda bi, p: (0, 0))]
    yl, yc = pl.pallas_call(
        functools.partial(_mlstm_kernel, n_lat=t // ML_CHUNK, n_ctx=tc // ML_CHUNK),
        grid=(b, n_pairs),
        in_specs=in_specs,
        out_specs=[pl.BlockSpec((None, t, 2 * LANES), lambda bi, p: (bi, 0, p)),
                   pl.BlockSpec((None, tc, 2 * LANES), lambda bi, p: (bi, 0, p))],
        out_shape=[jax.ShapeDtypeStruct((b, t, ML_HEADS * LANES), F32),
                   jax.ShapeDtypeStruct((b, tc, ML_HEADS * LANES), F32)],
        scratch_shapes=[pltpu.VMEM((t, 2 * LANES), F32), pltpu.VMEM((tc, 2 * LANES), F32),
                        pltpu.VMEM((4, LANES, 2 * LANES), F32), pltpu.VMEM((4, 8, LANES), F32)],
        compiler_params=_params("arbitrary", "arbitrary"),
        name="mlstm",
    )(*([pl_f32] * 5 + [pc_f32] * 5 + [gate_bias, ng.reshape(1, LANES)]))
    return yl, yc


def _unit_tri_inverse(a, eye, blk16, blk32):
    d = jnp.where(blk16, a, 0.0)
    x = eye - d
    p = _dot_x3(d, d)
    x = x + _dot_x3(x, p)
    p = _dot_x3(p, p)
    x = x + _dot_x3(x, p)
    p = _dot_x3(p, p)
    x = x + _dot_x3(x, p)
    e = jnp.where(blk32 & jnp.logical_not(blk16), a, 0.0)
    x = x - _dot_x3(x, _dot_x3(e, x))
    e = jnp.where(blk32, 0.0, a)
    x = x - _dot_x3(x, _dot_x3(e, x))
    return x


def _dn_kernel(ql_ref, kl_ref, vl_ref, zl_ref, gl_ref, qc_ref, kc_ref, vc_ref, zc_ref, gc_ref,
               wq_ref, wk_ref, wv_ref, alog_ref, dtb_ref, ng_ref, yl_ref, yc_ref,
               sql_ref, skl_ref, svl_ref, sqc_ref, skc_ref, svc_ref, obl_ref, obc_ref, st_ref, *, n_lat, n_ctx):
    c = DN_CHUNK
    lane = lax.broadcasted_iota(jnp.int32, (1, LANES), 1)
    ri = lax.broadcasted_iota(jnp.int32, (c, c), 0)
    ci = lax.broadcasted_iota(jnp.int32, (c, c), 1)
    incl = (ci <= ri, ci >= ri)
    strict = (ci < ri, ci > ri)
    cums = tuple(_bf(m.astype(F32)) for m in incl)
    eye = (ci == ri).astype(F32)
    blk16 = (ri >> 4) == (ci >> 4)
    blk32 = (ri >> 5) == (ci >> 5)
    neg_a = -jnp.exp(alog_ref[...])
    dtb = dtb_ref[...]

    def conv_prep(x_ref, w_ref, dst_ref, kind):
        n = x_ref.shape[0]
        x = x_ref[...]
        rows = lax.broadcasted_iota(jnp.int32, (n, LANES), 0)
        prev = jnp.where(rows == 0, 0.0, pltpu.roll(x, 1, 0))
        nxt = jnp.where(rows == n - 1, 0.0, pltpu.roll(x, n - 1, 0))
        w = w_ref[...]
        y = _silu(prev * w[0:1, :] + x * w[1:2, :] + nxt * w[2:3, :])
        if kind != "v":
            y = y * lax.rsqrt(jnp.sum(y * y, axis=-1, keepdims=True) + EPS)
        if kind == "q":
            y = y * (LANES ** -0.5)
        dst_ref[...] = y

    conv_prep(qc_ref, wq_ref, sqc_ref, "q")
    conv_prep(kc_ref, wk_ref, skc_ref, "k")
    conv_prep(vc_ref, wv_ref, svc_ref, "v")
    conv_prep(ql_ref, wq_ref, sql_ref, "q")
    conv_prep(kl_ref, wk_ref, skl_ref, "k")
    conv_prep(vl_ref, wv_ref, svl_ref, "v")
    st_ref[...] = jnp.zeros_like(st_ref)

    def chunk_step(q_ref, k_ref, v_ref, g_ref, dst_refs, rows):
        for d in range(2):
            r0 = rows[d]
            g = g_ref[pl.ds(r0, c), :]
            beta = _sigmoid(g)[:, 2 * d:2 * d + 1]
            gdec = neg_a * _softplus(g + dtb)
            cum = _sel_dot(cums[d], gdec)
            c_a = 2 * d + 1
            gcum = cum[:, c_a:c_a + 1]
            g_end = jnp.sum(gdec[:, c_a:c_a + 1], axis=0, keepdims=True)
            sel = jnp.broadcast_to(_bf((lane == c_a).astype(F32)), (c, LANES))
            g_row = _sel_dot_nt(sel, cum)
            decay_ts = jnp.exp(jnp.where(incl[d], gcum - g_row, -jnp.inf))
            q = q_ref[pl.ds(r0, c), :]
            k = k_ref[pl.ds(r0, c), :]
            v = v_ref[pl.ds(r0, c), :]
            qb, kb = _bf(q), _bf(k)
            a = jnp.where(strict[d], beta * decay_ts * _dot_nt(kb, kb), 0.0)
            tinv = _unit_tri_inverse(a, eye, blk16, blk32)
            gam = jnp.exp(gcum)
            rhs = jnp.concatenate([beta * v, (beta * gam) * k], axis=1)
            sol = _dot_x3(tinv, rhs)
            s_st = st_ref[d]
            sb = _bf(s_st)
            u = sol[:, :LANES] - _dot(_bf(sol[:, LANES:]), sb)
            ub = _bf(u)
            attn = _dot_nt(qb, kb) * decay_ts
            dst_refs[d][pl.ds(r0, c), :] = gam * _dot(qb, sb) + _dot(_bf(attn), ub)
            kdec = _bf(k * jnp.exp(g_end - gcum))
            st_ref[d] = jnp.exp(g_end) * s_st + _dot_tn(kdec, ub)

    def ctx_body(cc, carry):
        rows = (pl.multiple_of(cc * c, c), pl.multiple_of((n_ctx - 1 - cc) * c, c))
        chunk_step(sqc_ref, skc_ref, svc_ref, gc_ref, (yc_ref, obc_ref), rows)
        return carry

    lax.fori_loop(0, n_ctx, ctx_body, 0)

    def lat_body(cc, carry):
        rows = (pl.multiple_of(cc * c, c), pl.multiple_of((n_lat - 1 - cc) * c, c))
        chunk_step(sql_ref, skl_ref, svl_ref, gl_ref, (yl_ref, obl_ref), rows)
        return carry

    lax.fori_loop(0, n_lat, lat_body, 0)

    ng = ng_ref[...]

    def finish(y_ref, ob_ref, z_ref):
        n_rows = y_ref.shape[0]
        tile = 256
        for r in range(0, n_rows, tile):
            sl = slice(r, r + tile)
            y_ref[sl, :] = _rms(y_ref[sl, :] + ob_ref[sl, :], ng) * _silu(z_ref[sl, :])

    finish(yc_ref, obc_ref, zc_ref)
    finish(yl_ref, obl_ref, zl_ref)


def _deltanet(pl_f32, pc_f32, conv_w, alog_rows, dtb_rows, ng):
    b, t, _ = pl_f32.shape
    tc = pc_f32.shape[1]
    h = DN_HEADS

    def specs(n):
        return [pl.BlockSpec((None, n, LANES), lambda bi, hi: (bi, 0, hi)),
                pl.BlockSpec((None, n, LANES), lambda bi, hi: (bi, 0, h + hi)),
                pl.BlockSpec((None, n, LANES), lambda bi, hi: (bi, 0, 2 * h + hi)),
                pl.BlockSpec((None, n, LANES), lambda bi, hi: (bi, 0, 3 * h + hi)),
                pl.BlockSpec((None, n, LANES), lambda bi, hi: (bi, 0, 4 * h + hi))]
    in_specs = specs(t) + specs(tc) + [
        pl.BlockSpec((3, LANES), lambda bi, hi: (0, hi)),
        pl.BlockSpec((3, LANES), lambda bi, hi: (0, h + hi)),
        pl.BlockSpec((3, LANES), lambda bi, hi: (0, 2 * h + hi)),
        pl.BlockSpec((None, 1, LANES), lambda bi, hi: (hi, 0, 0)),
        pl.BlockSpec((None, 1, LANES), lambda bi, hi: (hi, 0, 0)),
        pl.BlockSpec((1, LANES), lambda bi, hi: (0, 0))]
    yl, yc = pl.pallas_call(
        functools.partial(_dn_kernel, n_lat=t // DN_CHUNK, n_ctx=tc // DN_CHUNK),
        grid=(b, h),
        in_specs=in_specs,
        out_specs=[pl.BlockSpec((None, t, LANES), lambda bi, hi: (bi, 0, hi)),
                   pl.BlockSpec((None, tc, LANES), lambda bi, hi: (bi, 0, hi))],
        out_shape=[jax.ShapeDtypeStruct((b, t, h * LANES), F32),
                   jax.ShapeDtypeStruct((b, tc, h * LANES), F32)],
        scratch_shapes=[pltpu.VMEM((t, LANES), F32)] * 3 + [pltpu.VMEM((tc, LANES), F32)] * 3
                       + [pltpu.VMEM((t, LANES), F32), pltpu.VMEM((tc, LANES), F32),
                          pltpu.VMEM((2, LANES, LANES), F32)],
        compiler_params=_params("arbitrary", "arbitrary"),
        name="deltanet",
    )(*([pl_f32] * 5 + [pc_f32] * 5 + [conv_w, conv_w, conv_w, alog_rows, dtb_rows, ng.reshape(1, LANES)]))
    return yl, yc


_FF_CHUNKS = ((0, 1536), (1536, D_FF))


def _post_kernel(*refs, final):
    if final:
        x_ref, ya_ref, yb_ref, mod_ref, gf_ref, wa_ref, wb_ref, wg_ref, wu_ref, wd_ref, gfin_ref, o_ref = refs
    else:
        x_ref, ya_ref, yb_ref, mod_ref, gf_ref, wa_ref, wb_ref, wg_ref, wu_ref, wd_ref, o_ref = refs
    mod = mod_ref[...]
    ga1 = mod[:, 2 * D_MODEL:3 * D_MODEL]
    ga2 = mod[:, 5 * D_MODEL:6 * D_MODEL]
    y = _dot(_bf(ya_ref[...]), wa_ref[...]) + _dot(_bf(yb_ref[...]), wb_ref[...])
    x1 = x_ref[...] + ga1 * y
    hb = _bf(_norm_mod(x1, gf_ref[...], mod, 3))
    acc = None
    for lo, hi in _FF_CHUNKS:
        gate = _dot(hb, wg_ref[:, lo:hi])
        up = _dot(hb, wu_ref[:, lo:hi])
        part = _dot(_bf(_silu(gate) * up), wd_ref[lo:hi, :])
        acc = part if acc is None else acc + part
    out = x1 + ga2 * acc
    if final:
        out = _rms(out, gfin_ref[...])
    o_ref[...] = out


def _post(x, ya, yb, mods, gf, wa, wb, wg, wu, wd, gfin, *, mod_row, tm):
    b, t, _ = x.shape
    rows = b * t
    nb = rows // tm
    final = gfin is not None
    na, nbw = ya.shape[-1], yb.shape[-1]

    def const(shape):
        return pl.BlockSpec(shape, lambda i: (0, 0), pipeline_mode=pl.Buffered(1))
    in_specs = [pl.BlockSpec((tm, D_MODEL), lambda i: (i, 0)),
                pl.BlockSpec((tm, na), lambda i: (i, 0)),
                pl.BlockSpec((tm, nbw), lambda i: (i, 0)),
                pl.BlockSpec((None, 1, 6 * D_MODEL), lambda i: (mod_row(i), 0, 0)),
                pl.BlockSpec((1, D_MODEL), lambda i: (0, 0)),
                const((na, D_MODEL)), const((nbw, D_MODEL)),
                const((D_MODEL, D_FF)), const((D_MODEL, D_FF)), const((D_FF, D_MODEL))]
    args = [x.reshape(rows, D_MODEL), ya.reshape(rows, na), yb.reshape(rows, nbw), mods,
            gf.reshape(1, D_MODEL), wa, wb, wg, wu, wd]
    if final:
        in_specs.append(pl.BlockSpec((1, D_MODEL), lambda i: (0, 0)))
        args.append(gfin.reshape(1, D_MODEL))
    out = pl.pallas_call(
        functools.partial(_post_kernel, final=final),
        grid=(nb,),
        in_specs=in_specs,
        out_specs=pl.BlockSpec((tm, D_MODEL), lambda i: (i, 0)),
        out_shape=jax.ShapeDtypeStruct((rows, D_MODEL), F32),
        compiler_params=_params("arbitrary"),
        name="post_ffn",
    )(*args)
    return out.reshape(b, t, D_MODEL)


def _rope_tables(n_tok):
    rows = n_tok // GRID_W
    quarter = HD // 4
    freqs = ROPE_THETA ** (-jnp.arange(quarter, dtype=F32) / quarter)
    row = jnp.repeat(jnp.arange(rows, dtype=F32), GRID_W)
    col = jnp.tile(jnp.arange(GRID_W, dtype=F32), rows)
    ang = jnp.concatenate([row[:, None] * freqs, col[:, None] * freqs], axis=-1)
    cos, sin = jnp.cos(ang), jnp.sin(ang)
    lane = np.arange(LANES)
    d = lane % HD
    src = (d // 32) * quarter + (d % quarter)
    first = (d % 32) < quarter
    c = cos[:, src]
    s = sin[:, src]
    s1 = jnp.where(first[None, :], -s, 0.0)
    s2 = jnp.where(first[None, :], 0.0, s)
    return c, s1, s2


def _place(cols_src, width, dst_lanes):
    idx = np.zeros((width,), np.int32)
    msk = np.zeros((width,), bool)
    for src, dst in zip(cols_src, dst_lanes):
        idx[dst] = src
        msk[dst] = True
    return idx, msk


def _gather_cols(w, idx, msk):
    return jnp.where(jnp.asarray(msk)[None, :], w[:, jnp.asarray(idx)], 0.0)


def _even_layout():
    src, dst = [], []
    for i in range(3072):
        src.append(i)
        dst.append(i)
    gbase = 3072
    for p in range(2):
        for kind in range(4):
            for hh in range(2):
                src.append(gbase + kind * ML_HEADS + 2 * p + hh)
                dst.append(3072 + p * LANES + kind * 2 + hh)
    return _place(src, EVEN_BF + EVEN_F32, dst)


def _odd_layout():
    src, dst = [], []
    per_kv = GQ_HEADS // GQ_KV
    for h in range(GQ_HEADS):
        j = h // per_kv
        for e in range(HD):
            src.append(h * HD + e)
            dst.append(h * LANES + j * HD + e)
    for j in range(GQ_KV):
        for e in range(HD):
            src.append(512 + j * HD + e)
            dst.append(GQ_HEADS * LANES + j * LANES + j * HD + e)
    for e in range(GQ_KV * HD):
        src.append(640 + e)
        dst.append((GQ_HEADS + GQ_KV) * LANES + e)
    for e in range(1536 + 512):
        src.append(768 + e)
        dst.append(ODD_BF + e)
    for h in range(DN_HEADS):
        for d in range(2):
            for kind in range(2):
                src.append(2816 + d * 2 * DN_HEADS + kind * DN_HEADS + h)
                dst.append(ODD_BF + 2048 + h * LANES + d * 2 + kind)
    return _place(src, ODD_BF + ODD_F32, dst)


def kernel(x, c, ctx, c_ctx, w_ada, b_ada, g_mix, g_ffn, w_in_e, b_gate_e, da_lam, da_norm_g, ml_norm_g,
           w_out_e, w_in_o, qk_norm_g, dn_conv, dn_a_log, dn_dt_bias, dn_norm_g, w_out_o, w_gate, w_up,
           w_down, g_final):
    b, t, _ = x.shape
    tc = ctx.shape[1]
    tables = _rope_tables(t)

    cv = jnp.zeros((16, D_MODEL), F32).at[:b].set(c).at[b].set(c_ctx)
    mods = _ada_mods(cv, w_ada, b_ada).reshape(DEPTH, 16, 1, 6 * D_MODEL)

    e_idx, e_msk = _even_layout()
    o_idx, o_msk = _odd_layout()
    tm_l = 512
    tm_c = 256
    bpb = t // tm_l
    lat_row = lambda i: i // bpb
    ctx_row = lambda i: b

    xl, xc = x, ctx
    for layer in range(DEPTH):
        emit_ctx = layer < DEPTH - 1
        m = mods[layer]
        if layer % 2 == 0:
            e = layer // 2
            lam_init = 0.8 - 0.6 * math.exp(-0.3 * layer)
            w = _bf(_gather_cols(w_in_e[e], e_idx, e_msk))
            pl_bf, pl_f = _inproj(xl, m, g_mix[layer], w, [], tables, even=True, seq_len=t, mod_row=lat_row, tm=tm_l)
            pc_bf, pc_f = _inproj(xc, m, g_mix[layer], w, [], None, even=True, seq_len=tc, mod_row=ctx_row, tm=tm_c)
            ya_l = _diff_attn(pl_bf, [pc_bf, pl_bf], da_lam[e], da_norm_g[e], lam_init, 256, True)
            gbias = jnp.zeros((2, 1, LANES), F32)
            gb = b_gate_e[e].reshape(4, 2, 2)
            gbias = gbias.at[:, 0, :8].set(gb.transpose(1, 0, 2).reshape(2, 8))
            yb_l, yb_c = _mlstm(pl_f, pc_f, gbias, ml_norm_g[e])
            if emit_ctx:
                ya_c = _diff_attn(pc_bf, [pc_bf], da_lam[e], da_norm_g[e], lam_init, tc, False)
            w_out = _bf(w_out_e[e])
            wa, wb = w_out[:DA_HEADS * LANES], w_out[DA_HEADS * LANES:]
        else:
            o = layer // 2
            w = _bf(_gather_cols(w_in_o[o], o_idx, o_msk))
            qkg = jnp.zeros((2 * GQ_KV, LANES), F32)
            for j in range(GQ_KV):
                qkg = qkg.at[j, j * HD:(j + 1) * HD].set(qk_norm_g[o, 0])
                qkg = qkg.at[GQ_KV + j, j * HD:(j + 1) * HD].set(qk_norm_g[o, 1])
            pl_bf, pl_f = _inproj(xl, m, g_mix[layer], w, [qkg], tables, even=False, seq_len=t, mod_row=lat_row, tm=tm_l)
            pc_bf, pc_f = _inproj(xc, m, g_mix[layer], w, [qkg], None, even=False, seq_len=tc, mod_row=ctx_row, tm=tm_c)
            ya_l = _gqa_attn(pl_bf, [pc_bf, pl_bf], 256, True)
            alog = jnp.zeros((DN_HEADS, 1, LANES), F32)
            dtb = jnp.zeros((DN_HEADS, 1, LANES), F32)
            for d in range(2):
                alog = alog.at[:, 0, 2 * d + 1].set(dn_a_log[o, d])
                dtb = dtb.at[:, 0, 2 * d + 1].set(dn_dt_bias[o, d])
            yb_l, yb_c = _deltanet(pl_f, pc_f, dn_conv[o], alog, dtb, dn_norm_g[o])
            if emit_ctx:
                ya_c = _gqa_attn(pc_bf, [pc_bf], tc, False)
            w_out = _bf(w_out_o[o])
            wa, wb = w_out[:GQ_HEADS * HD], w_out[GQ_HEADS * HD:]
        wg, wu, wd = _bf(w_gate[layer]), _bf(w_up[layer]), _bf(w_down[layer])
        gfin = g_final if layer == DEPTH - 1 else None
        xl = _post(xl, ya_l, yb_l, m, g_ffn[layer], wa, wb, wg, wu, wd, gfin, mod_row=lat_row, tm=tm_l)
        if emit_ctx:
            xc = _post(xc, ya_c, yb_c, m, g_ffn[layer], wa, wb, wg, wu, wd, None, mod_row=ctx_row, tm=tm_c)
    return xl
```

```python
import functools
import math

import jax
import jax.numpy as jnp
import numpy as np
from jax import lax
from jax.experimental import pallas as pl
from jax.experimental.pallas import tpu as pltpu

F32 = jnp.float32
BF16 = jnp.bfloat16

D_MODEL = 1024
DEPTH = 4
GRID_W = 64
ROPE_THETA = 10000.0
EPS = 1e-6
HD = 64
Q_BLOCK = 128
LANES = 128
DA_HEADS = 4
ML_HEADS = 4
GQ_HEADS = 8
GQ_KV = 2
DN_HEADS = 4
D_FF = 2816
ML_CHUNK = 128
DN_CHUNK = 64
DN_GROUP = 4
VMEM_LIMIT = 56 * 1024 * 1024

EVEN_BF = 1536
EVEN_F32 = 1792
ODD_BF = 1408
ODD_F32 = 2560


def _bf(x):
    return x.astype(BF16)


def _dot(a, b):
    return jnp.dot(a, b, preferred_element_type=F32)


def _dot_nt(a, b):
    return lax.dot_general(a, b, (((1,), (1,)), ((), ())), preferred_element_type=F32)


def _dot_tn(a, b):
    return lax.dot_general(a, b, (((0,), (0,)), ((), ())), preferred_element_type=F32)


def _split2(x):
    hi = _bf(x)
    return hi, _bf(x - hi.astype(F32))


def _split3(x):
    hi = _bf(x)
    r = x - hi.astype(F32)
    mid = _bf(r)
    return hi, mid, _bf(r - mid.astype(F32))


def _dot_x3(a, b):
    ah, al = _split2(a)
    bh, bl = _split2(b)
    return _dot(ah, bh) + (_dot(ah, bl) + _dot(al, bh))


def _sel_dot(sel, x):
    h, m, l = _split3(x)
    return _dot(sel, h) + (_dot(sel, m) + _dot(sel, l))


def _sel_dot_nt(sel, x):
    h, m, l = _split3(x)
    return _dot_nt(sel, h) + (_dot_nt(sel, m) + _dot_nt(sel, l))


def _sigmoid(x):
    return 1.0 / (1.0 + jnp.exp(-x))


def _silu(x):
    return x * _sigmoid(x)


def _log_sigmoid(x):
    return jnp.minimum(x, 0.0) - jnp.log(1.0 + jnp.exp(-jnp.abs(x)))


def _softplus(x):
    return jnp.maximum(x, 0.0) + jnp.log(1.0 + jnp.exp(-jnp.abs(x)))


def _rms(x, g, n=None):
    n = x.shape[-1] if n is None else n
    ss = jnp.sum(x * x, axis=-1, keepdims=True)
    return x * lax.rsqrt(ss * (1.0 / n) + EPS) * g


def _norm_mod(x, g, mod, slot):
    sh = mod[:, slot * D_MODEL:(slot + 1) * D_MODEL]
    sc = mod[:, (slot + 1) * D_MODEL:(slot + 2) * D_MODEL]
    return _rms(x, g) * (1.0 + sc) + sh


def _rope(x, c, s1, s2):
    return x * c + pltpu.roll(x, LANES - 16, 1) * s1 + pltpu.roll(x, 16, 1) * s2


def _params(*sem):
    return pltpu.CompilerParams(dimension_semantics=sem, vmem_limit_bytes=VMEM_LIMIT)


def _ada_kernel(cv_ref, w_ref, b_ref, o_ref):
    cv = cv_ref[...]
    o_ref[...] = _dot_x3(_silu(cv), w_ref[...]) + b_ref[...]


def _ada_mods(cv, w_ada, b_ada):
    rows = cv.shape[0]
    nb = 6
    return pl.pallas_call(
        _ada_kernel,
        grid=(DEPTH, nb),
        in_specs=[pl.BlockSpec((rows, D_MODEL), lambda l, j: (0, 0)),
                  pl.BlockSpec((None, D_MODEL, D_MODEL), lambda l, j: (l, 0, j)),
                  pl.BlockSpec((None, 1, D_MODEL), lambda l, j: (l, 0, j))],
        out_specs=pl.BlockSpec((None, rows, D_MODEL), lambda l, j: (l, 0, j)),
        out_shape=jax.ShapeDtypeStruct((DEPTH, rows, 6 * D_MODEL), F32),
        compiler_params=_params("arbitrary", "arbitrary"),
        name="ada_mods",
    )(cv, w_ada, b_ada.reshape(DEPTH, 1, 6 * D_MODEL))


def _inproj_even_kernel(*refs, rope):
    if rope:
        x_ref, mod_ref, g_ref, w_ref, rc_ref, s1_ref, s2_ref, o1_ref, o2_ref = refs
    else:
        x_ref, mod_ref, g_ref, w_ref, o1_ref, o2_ref = refs
    hb = _bf(_norm_mod(x_ref[...], g_ref[...], mod_ref[...], 0))
    for seg in range(4):
        p = _dot(hb, w_ref[:, seg * 256:(seg + 1) * 256])
        for half in range(2):
            gi = seg * 2 + half
            xg = p[:, half * LANES:(half + 1) * LANES]
            if rope:
                xg = _rope(xg, rc_ref[...], s1_ref[...], s2_ref[...])
            if gi < DA_HEADS:
                xg = xg * (HD ** -0.5)
            o1_ref[:, gi * LANES:(gi + 1) * LANES] = _bf(xg)
    o1_ref[:, 1024:EVEN_BF] = _bf(_dot(hb, w_ref[:, 1024:EVEN_BF]))
    o2_ref[...] = _dot(hb, w_ref[:, EVEN_BF:])


def _inproj_odd_kernel(*refs, rope):
    if rope:
        x_ref, mod_ref, g_ref, w_ref, qkg_ref, rc_ref, s1_ref, s2_ref, o1_ref, o2_ref = refs
    else:
        x_ref, mod_ref, g_ref, w_ref, qkg_ref, o1_ref, o2_ref = refs
    hb = _bf(_norm_mod(x_ref[...], g_ref[...], mod_ref[...], 0))
    for seg in range(5):
        p = _dot(hb, w_ref[:, seg * 256:(seg + 1) * 256])
        for half in range(2):
            gi = seg * 2 + half
            xg = p[:, half * LANES:(half + 1) * LANES]
            if gi < GQ_HEADS:
                grow = gi // (GQ_HEADS // GQ_KV)
            else:
                grow = GQ_KV + (gi - GQ_HEADS)
            xg = _rms(xg, qkg_ref[grow:grow + 1, :], HD)
            if rope:
                xg = _rope(xg, rc_ref[...], s1_ref[...], s2_ref[...])
            if gi < GQ_HEADS:
                xg = xg * (HD ** -0.5)
            o1_ref[:, gi * LANES:(gi + 1) * LANES] = _bf(xg)
    o1_ref[:, 1280:ODD_BF] = _bf(_dot(hb, w_ref[:, 1280:ODD_BF]))
    o2_ref[...] = _dot(hb, w_ref[:, ODD_BF:])


def _inproj(x, mods, g, w, extra, tables, *, even, seq_len, mod_row, tm):
    b, t, _ = x.shape
    rows = b * t
    nb = rows // tm
    bpb = max(t // tm, 1)
    n_bf, n_f32 = (EVEN_BF, EVEN_F32) if even else (ODD_BF, ODD_F32)
    rope = tables is not None
    kern = functools.partial(_inproj_even_kernel if even else _inproj_odd_kernel, rope=rope)
    in_specs = [pl.BlockSpec((tm, D_MODEL), lambda i: (i, 0)),
                pl.BlockSpec((None, 1, 6 * D_MODEL), lambda i: (mod_row(i), 0, 0)),
                pl.BlockSpec((1, D_MODEL), lambda i: (0, 0)),
                pl.BlockSpec((D_MODEL, n_bf + n_f32), lambda i: (0, 0))]
    args = [x.reshape(rows, D_MODEL), mods, g.reshape(1, D_MODEL), w]
    for e in extra:
        in_specs.append(pl.BlockSpec(e.shape, lambda i: (0, 0)))
        args.append(e)
    if rope:
        for tab in tables:
            in_specs.append(pl.BlockSpec((tm, LANES), lambda i: (i % bpb, 0)))
            args.append(tab)
    o1, o2 = pl.pallas_call(
        kern,
        grid=(nb,),
        in_specs=in_specs,
        out_specs=[pl.BlockSpec((tm, n_bf), lambda i: (i, 0)),
                   pl.BlockSpec((tm, n_f32), lambda i: (i, 0))],
        out_shape=[jax.ShapeDtypeStruct((rows, n_bf), BF16),
                   jax.ShapeDtypeStruct((rows, n_f32), F32)],
        compiler_params=_params("arbitrary"),
        name="inproj_even" if even else "inproj_odd",
    )(*args)
    return o1.reshape(b, t, n_bf), o2.reshape(b, t, n_f32)


def _softmax_pv(q, kvs):
    ss = [_dot_nt(q, k) for k, _ in kvs]
    mx = ss[0].max(axis=-1, keepdims=True)
    for s in ss[1:]:
        mx = jnp.maximum(mx, s.max(axis=-1, keepdims=True))
    den = None
    acc = None
    for s, (_, v) in zip(ss, kvs):
        p = jnp.exp(s - mx)
        l = p.sum(axis=-1, keepdims=True)
        o = _dot(_bf(p), v)
        den = l if den is None else den + l
        acc = o if acc is None else acc + o
    return acc / den


def _store_heads(o_ref, o, group, width, blocked):
    if not blocked:
        o_ref[:, group * LANES:(group + 1) * LANES] = o
        return
    for jj in range(o.shape[0] // Q_BLOCK):
        col = jj * width + group * LANES
        o_ref[:, col:col + LANES] = o[jj * Q_BLOCK:(jj + 1) * Q_BLOCK, :]


def _attn_out(b, t, width, tq, blocked):
    if blocked:
        nblk = t // Q_BLOCK
        spec = pl.BlockSpec((None, Q_BLOCK, (tq // Q_BLOCK) * width), lambda bi, qi: (bi, 0, qi))
        return spec, jax.ShapeDtypeStruct((b, Q_BLOCK, nblk * width), F32)
    return (pl.BlockSpec((None, tq, width), lambda bi, qi: (bi, qi, 0)),
            jax.ShapeDtypeStruct((b, t, width), F32))


def _diff_attn_kernel(*refs, n_kv, lam_init, blocked):
    q_ref = refs[0]
    kv_refs = refs[1:1 + 2 * n_kv]
    lam_ref, ng_ref, o_ref = refs[1 + 2 * n_kv:]
    lane = lax.broadcasted_iota(jnp.int32, (1, LANES), 1)
    lf = lam_ref[...]
    s1 = jnp.sum(lf[0:1, :] * lf[1:2, :], axis=-1, keepdims=True)
    s2 = jnp.sum(lf[2:3, :] * lf[3:4, :], axis=-1, keepdims=True)
    lmb = jnp.exp(s1) - jnp.exp(s2) + lam_init
    ng = ng_ref[...]
    for h in range(DA_HEADS):
        hs = slice(h * LANES, (h + 1) * LANES)
        q = q_ref[:, hs]
        kvs = [(kv_refs[2 * i][:, hs], kv_refs[2 * i + 1][:, hs]) for i in range(n_kv)]
        outs = []
        for m in range(2):
            in_map = (lane >= m * HD) & (lane < (m + 1) * HD)
            qm = jnp.where(in_map, q, jnp.zeros_like(q))
            outs.append(_softmax_pv(qm, kvs))
        o = outs[0] - lmb * outs[1]
        _store_heads(o_ref, _rms(o, ng) * (1.0 - lam_init), h, DA_HEADS * LANES, blocked)


def _diff_attn(q_src, kv_srcs, lam, ng, lam_init, tq, blocked):
    b, t, _ = q_src.shape
    nq = t // tq
    width = DA_HEADS * LANES
    in_specs = [pl.BlockSpec((None, tq, width), lambda bi, qi: (bi, qi, 0))]
    args = [q_src]
    for src in kv_srcs:
        n = src.shape[1]
        in_specs.append(pl.BlockSpec((None, n, width), lambda bi, qi: (bi, 0, 1)))
        in_specs.append(pl.BlockSpec((None, n, width), lambda bi, qi: (bi, 0, 2)))
        args += [src, src]
    in_specs += [pl.BlockSpec((4, HD), lambda bi, qi: (0, 0)),
                 pl.BlockSpec((1, LANES), lambda bi, qi: (0, 0))]
    args += [lam, ng.reshape(1, LANES)]
    out_spec, out_shape = _attn_out(b, t, width, tq, blocked)
    out = pl.pallas_call(
        functools.partial(_diff_attn_kernel, n_kv=len(kv_srcs), lam_init=lam_init, blocked=blocked),
        grid=(b, nq),
        in_specs=in_specs,
        out_specs=out_spec,
        out_shape=out_shape,
        compiler_params=_params("arbitrary", "arbitrary"),
        name="diff_attn",
    )(*args)
    return out.reshape(b, t, width)


def _gqa_kernel(*refs, n_kv, blocked):
    q_ref = refs[0]
    kv_refs = refs[1:1 + 2 * n_kv]
    o_ref = refs[1 + 2 * n_kv]
    lane = lax.broadcasted_iota(jnp.int32, (1, LANES), 1)
    low = lane < HD
    per_kv = GQ_HEADS // GQ_KV
    for pair in range(GQ_HEADS // 2):
        j = (2 * pair) // per_kv
        kvs = [(kv_refs[2 * i][:, j * LANES:(j + 1) * LANES], kv_refs[2 * i + 1][...]) for i in range(n_kv)]
        o_a = _softmax_pv(q_ref[:, (2 * pair) * LANES:(2 * pair + 1) * LANES], kvs)
        o_b = _softmax_pv(q_ref[:, (2 * pair + 1) * LANES:(2 * pair + 2) * LANES], kvs)
        if j == 0:
            packed = jnp.where(low, o_a, pltpu.roll(o_b, HD, 1))
        else:
            packed = jnp.where(low, pltpu.roll(o_a, HD, 1), o_b)
        _store_heads(o_ref, packed, pair, GQ_HEADS * HD, blocked)


def _gqa_attn(q_src, kv_srcs, tq, blocked):
    b, t, _ = q_src.shape
    nq = t // tq
    qw = GQ_HEADS * LANES
    in_specs = [pl.BlockSpec((None, tq, qw), lambda bi, qi: (bi, qi, 0))]
    args = [q_src]
    for src in kv_srcs:
        n = src.shape[1]
        in_specs.append(pl.BlockSpec((None, n, GQ_KV * LANES), lambda bi, qi: (bi, 0, qw // (GQ_KV * LANES))))
        in_specs.append(pl.BlockSpec((None, n, LANES), lambda bi, qi: (bi, 0, (qw + GQ_KV * LANES) // LANES)))
        args += [src, src]
    out_spec, out_shape = _attn_out(b, t, GQ_HEADS * HD, tq, blocked)
    out = pl.pallas_call(
        functools.partial(_gqa_kernel, n_kv=len(kv_srcs), blocked=blocked),
        grid=(b, nq),
        in_specs=in_specs,
        out_specs=out_spec,
        out_shape=out_shape,
        compiler_params=_params("arbitrary", "arbitrary"),
        name="gqa_attn",
    )(*args)
    return out.reshape(b, t, GQ_HEADS * HD)


def _mlstm_kernel(ql_ref, kl_ref, vl_ref, ol_ref, gl_ref, qc_ref, kc_ref, vc_ref, oc_ref, gc_ref,
                  gb_ref, ng_ref, yl_ref, yc_ref, hbl_ref, hbc_ref, cst_ref, mst_ref, *, n_lat, n_ctx):
    c = ML_CHUNK
    lane = lax.broadcasted_iota(jnp.int32, (1, LANES), 1)
    ri = lax.broadcasted_iota(jnp.int32, (c, c), 0)
    ci = lax.broadcasted_iota(jnp.int32, (c, c), 1)
    masks = (ci <= ri, ci >= ri)
    cums = tuple(_bf(m.astype(F32)) for m in masks)
    ones_col = jnp.broadcast_to((lane == 0).astype(F32), (c, LANES))
    gb = gb_ref[...]

    cst_ref[...] = jnp.zeros_like(cst_ref)
    mst_ref[...] = jnp.zeros_like(mst_ref)

    def chunk_step(q_ref, k_ref, v_ref, g_ref, dst_refs, rows):
        for d in range(2):
            r0 = rows[d]
            g = g_ref[pl.ds(r0, c), :] + gb
            lf = _log_sigmoid(g)
            cum = _sel_dot(cums[d], lf)
            q2 = q_ref[pl.ds(r0, c), :]
            k2 = k_ref[pl.ds(r0, c), :] * (HD ** -0.5)
            v2 = v_ref[pl.ds(r0, c), :]
            for hh in range(2):
                idx = d * 2 + hh
                c_i = d * 4 + hh
                c_f = d * 4 + 2 + hh
                i_col = g[:, c_i:c_i + 1]
                b_col = cum[:, c_f:c_f + 1]
                b_end = jnp.sum(lf[:, c_f:c_f + 1], axis=0, keepdims=True)
                sel_i = jnp.broadcast_to(_bf((lane == c_i).astype(F32)), (c, LANES))
                sel_f = jnp.broadcast_to(_bf((lane == c_f).astype(F32)), (c, LANES))
                i_row = _sel_dot_nt(sel_i, g)
                b_row = _sel_dot_nt(sel_f, cum)
                m_st = mst_ref[idx][0:1, 0:1]
                c_st = cst_ref[idx]
                in_head = (lane >= hh * HD) & (lane < (hh + 1) * HD)
                qm = _bf(jnp.where(in_head, q2, 0.0))
                km = _bf(jnp.where(in_head, k2, 0.0))
                vaug = jnp.concatenate([v2[:, hh * LANES:(hh + 1) * LANES], ones_col], axis=1)

                w_end = b_end - b_col + i_col
                m_new = jnp.maximum(b_end + m_st, jnp.max(w_end, axis=0, keepdims=True))
                decay = jnp.exp(b_end + m_st - m_new)
                w = jnp.exp(w_end - m_new)

                a = b_col + m_st
                dm = jnp.where(masks[d], b_col - b_row + i_row, -jnp.inf)
                mt = jnp.maximum(a, jnp.max(dm, axis=-1, keepdims=True))
                inter = jnp.exp(a - mt)
                p = jnp.exp(dm - mt) * _dot_nt(qm, km)
                nd = inter * _dot(qm, _bf(c_st)) + _dot(_bf(p), _bf(vaug))
                den = jnp.maximum(jnp.abs(nd[:, LANES:LANES + 1]), jnp.exp(-mt))
                dst_refs[d][pl.ds(r0, c), hh * LANES:(hh + 1) * LANES] = nd[:, :LANES] / den

                cst_ref[idx] = decay * c_st + _dot_tn(km, _bf(w * vaug))
                mst_ref[idx] = jnp.broadcast_to(m_new, mst_ref.shape[1:])

    for cc in range(n_ctx):
        chunk_step(qc_ref, kc_ref, vc_ref, gc_ref, (yc_ref, hbc_ref), (cc * c, (n_ctx - 1 - cc) * c))

    def body(cc, carry):
        rows = (pl.multiple_of(cc * c, c), pl.multiple_of((n_lat - 1 - cc) * c, c))
        chunk_step(ql_ref, kl_ref, vl_ref, gl_ref, (yl_ref, hbl_ref), rows)
        return carry

    lax.fori_loop(0, n_lat, body, 0)

    ng = ng_ref[...]

    def finish(y_ref, hb_ref, o_ref, n_rows):
        tile = 256
        for r in range(0, n_rows, tile):
            for hh in range(2):
                sl = (slice(r, r + tile), slice(hh * LANES, (hh + 1) * LANES))
                hsum = y_ref[sl] + hb_ref[sl]
                y_ref[sl] = _rms(hsum, ng) * _sigmoid(o_ref[sl])

    finish(yc_ref, hbc_ref, oc_ref, n_ctx * c)
    finish(yl_ref, hbl_ref, ol_ref, n_lat * c)


def _mlstm(pl_f32, pc_f32, gate_bias, ng):
    b, t, _ = pl_f32.shape
    tc = pc_f32.shape[1]
    n_pairs = ML_HEADS // 2
    def specs(n):
        return [pl.BlockSpec((None, n, LANES), lambda bi, p: (bi, 0, p)),
                pl.BlockSpec((None, n, LANES), lambda bi, p: (bi, 0, 2 + p)),
                pl.BlockSpec((None, n, 2 * LANES), lambda bi, p: (bi, 0, 2 + p)),
                pl.BlockSpec((None, n, 2 * LANES), lambda bi, p: (bi, 0, 4 + p)),
                pl.BlockSpec((None, n, LANES), lambda bi, p: (bi, 0, 12 + p))]
    in_specs = specs(t) + specs(tc) + [pl.BlockSpec((None, 1, LANES), lambda bi, p: (p, 0, 0)),
                                      pl.BlockSpec((1, LANES), lambda bi, p: (0, 0))]
    yl, yc = pl.pallas_call(
        functools.partial(_mlstm_kernel, n_lat=t // ML_CHUNK, n_ctx=tc // ML_CHUNK),
        grid=(b, n_pairs),
        in_specs=in_specs,
        out_specs=[pl.BlockSpec((None, t, 2 * LANES), lambda bi, p: (bi, 0, p)),
                   pl.BlockSpec((None, tc, 2 * LANES), lambda bi, p: (bi, 0, p))],
        out_shape=[jax.ShapeDtypeStruct((b, t, ML_HEADS * LANES), F32),
                   jax.ShapeDtypeStruct((b, tc, ML_HEADS * LANES), F32)],
        scratch_shapes=[pltpu.VMEM((t, 2 * LANES), F32), pltpu.VMEM((tc, 2 * LANES), F32),
                        pltpu.VMEM((4, LANES, 2 * LANES), F32), pltpu.VMEM((4, 8, LANES), F32)],
        compiler_params=_params("arbitrary", "arbitrary"),
        name="mlstm",
    )(*([pl_f32] * 5 + [pc_f32] * 5 + [gate_bias, ng.reshape(1, LANES)]))
    return yl, yc


def _each(f, *lists):
    return [f(*xs) for xs in zip(*lists)]


def _unit_tri_inverse(a, eye, blk16, blk32):
    d = [jnp.where(blk16, ai, 0.0) for ai in a]
    x = [eye - di for di in d]
    p = _each(_dot_x3, d, d)
    for step in range(3):
        x = _each(lambda xi, ti: xi + ti, x, _each(_dot_x3, x, p))
        if step < 2:
            p = _each(_dot_x3, p, p)
    e = [jnp.where(blk32 & jnp.logical_not(blk16), ai, 0.0) for ai in a]
    x = _each(lambda xi, ti: xi - ti, x, _each(_dot_x3, x, _each(_dot_x3, e, x)))
    e = [jnp.where(blk32, 0.0, ai) for ai in a]
    x = _each(lambda xi, ti: xi - ti, x, _each(_dot_x3, x, _each(_dot_x3, e, x)))
    return x


def _dn_kernel(ql_ref, kl_ref, vl_ref, zl_ref, gl_ref, qc_ref, kc_ref, vc_ref, zc_ref, gc_ref,
               wq_ref, wk_ref, wv_ref, alog_ref, dtb_ref, ng_ref, yl_ref, yc_ref,
               sql_ref, skl_ref, svl_ref, sqc_ref, skc_ref, svc_ref, obl_ref, obc_ref,
               p_ref, n_ref, qt_ref, o_ref, eg_ref, *, n_lat, n_ctx):
    c = DN_CHUNK
    lane = lax.broadcasted_iota(jnp.int32, (1, LANES), 1)
    ri = lax.broadcasted_iota(jnp.int32, (c, c), 0)
    ci = lax.broadcasted_iota(jnp.int32, (c, c), 1)
    incl = (ci <= ri, ci >= ri)
    strict = (ci < ri, ci > ri)
    cums = tuple(_bf(m.astype(F32)) for m in incl)
    eye = (ci == ri).astype(F32)
    blk16 = (ri >> 4) == (ci >> 4)
    blk32 = (ri >> 5) == (ci >> 5)
    neg_a = -jnp.exp(alog_ref[...])
    dtb = dtb_ref[...]

    def conv_prep(x_ref, w_ref, dst_ref, kind):
        n = x_ref.shape[0]
        x = x_ref[...]
        rows = lax.broadcasted_iota(jnp.int32, (n, LANES), 0)
        prev = jnp.where(rows == 0, 0.0, pltpu.roll(x, 1, 0))
        nxt = jnp.where(rows == n - 1, 0.0, pltpu.roll(x, n - 1, 0))
        w = w_ref[...]
        y = _silu(prev * w[0:1, :] + x * w[1:2, :] + nxt * w[2:3, :])
        if kind != "v":
            y = y * lax.rsqrt(jnp.sum(y * y, axis=-1, keepdims=True) + EPS)
        if kind == "q":
            y = y * (LANES ** -0.5)
        dst_ref[...] = y

    conv_prep(qc_ref, wq_ref, sqc_ref, "q")
    conv_prep(kc_ref, wk_ref, skc_ref, "k")
    conv_prep(vc_ref, wv_ref, svc_ref, "v")
    conv_prep(ql_ref, wq_ref, sql_ref, "q")
    conv_prep(kl_ref, wk_ref, skl_ref, "k")
    conv_prep(vl_ref, wv_ref, svl_ref, "v")
    sels = tuple(jnp.broadcast_to(_bf((lane == 2 * d + 1).astype(F32)), (c, LANES)) for d in range(2))

    def prep_group(q_ref, k_ref, v_ref, g_ref, c0, n_seq, base):
        chunks = range(DN_GROUP)
        units = [(u, d) for u in chunks for d in range(2)]
        rows = [pl.multiple_of((c0 + u) * c, c) for u in chunks]
        q = [q_ref[pl.ds(r, c), :] for r in rows]
        k = [k_ref[pl.ds(r, c), :] for r in rows]
        v = [v_ref[pl.ds(r, c), :] for r in rows]
        g = [g_ref[pl.ds(r, c), :] for r in rows]
        qb = [_bf(x) for x in q]
        kb = [_bf(x) for x in k]
        kk = _each(_dot_nt, kb, kb)
        qk = _each(_dot_nt, qb, kb)
        sig = [_sigmoid(x) for x in g]
        gdec = [neg_a * _softplus(x + dtb) for x in g]
        tot = [jnp.sum(x, axis=0, keepdims=True) for x in gdec]
        cum_f = [_sel_dot(cums[0], x) for x in gdec]
        cum = [jnp.where(lane < 2, cf, t - cf + x) for cf, t, x in zip(cum_f, tot, gdec)]
        gcum = [cum[u][:, 2 * d + 1:2 * d + 2] for u, d in units]
        g_end = [tot[u][:, 2 * d + 1:2 * d + 2] for u, d in units]
        beta = [sig[u][:, 2 * d:2 * d + 1] for u, d in units]
        g_row = [_sel_dot_nt(sels[d], cum[u]) for u, d in units]
        decay = [jnp.exp(jnp.where(incl[d], gc - gr, -jnp.inf)) for (u, d), gc, gr in zip(units, gcum, g_row)]
        a = [jnp.where(strict[d], bt * dc * kk[u], 0.0) for (u, d), bt, dc in zip(units, beta, decay)]
        tinv = _unit_tri_inverse(a, eye, blk16, blk32)
        gam = [jnp.exp(x) for x in gcum]
        rhs = [jnp.concatenate([bt * v[u], (bt * gm) * k[u]], axis=1) for (u, d), bt, gm in zip(units, beta, gam)]
        sol = [_bf(s) for s in _each(_dot_x3, tinv, rhs)]
        attn = [_bf(qk[u] * dc) for (u, d), dc in zip(units, decay)]
        kdec = [_bf(k[u] * jnp.exp(ge - gc)) for (u, d), ge, gc in zip(units, g_end, gcum)]
        qo = _each(_dot, attn, sol)
        pn = _each(_dot_tn, kdec, sol)
        for i, (u, d) in enumerate(units):
            chunk = c0 + u
            pos = base + (chunk if d == 0 else n_seq - 1 - chunk)
            p_ref[d, pos] = _bf(pn[i][:, LANES:])
            n_ref[d, pos] = pn[i][:, :LANES]
            qt_ref[d, pos] = _bf(gam[i] * q[u] - qo[i][:, LANES:])
            o_ref[d, pos] = qo[i][:, :LANES]
            eg_ref[d, pos] = jnp.broadcast_to(jnp.exp(g_end[i]), (8, LANES))

    for grp in range(n_ctx // DN_GROUP):
        prep_group(sqc_ref, skc_ref, svc_ref, gc_ref, grp * DN_GROUP, n_ctx, 0)

    def prep_body(grp, carry):
        prep_group(sql_ref, skl_ref, svl_ref, gl_ref, grp * DN_GROUP, n_lat, n_ctx)
        return carry

    lax.fori_loop(0, n_lat // DN_GROUP, prep_body, 0)

    def scan_body(dst_refs, n_seq, base):
        def body(i, states):
            pos = base + i
            rows = (pl.multiple_of(i * c, c), pl.multiple_of((n_seq - 1 - i) * c, c))
            new = []
            for d in range(2):
                s = states[d]
                sb = _bf(s)
                dst_refs[d][pl.ds(rows[d], c), :] = _dot(qt_ref[d, pos], sb) + o_ref[d, pos]
                new.append(eg_ref[d, pos][0:1, :] * s - _dot(p_ref[d, pos], sb) + n_ref[d, pos])
            return tuple(new)
        return body

    zero = jnp.zeros((LANES, LANES), F32)
    states = lax.fori_loop(0, n_ctx, scan_body((yc_ref, obc_ref), n_ctx, 0), (zero, zero))
    lax.fori_loop(0, n_lat, scan_body((yl_ref, obl_ref), n_lat, n_ctx), states)

    ng = ng_ref[...]

    def finish(y_ref, ob_ref, z_ref):
        n_rows = y_ref.shape[0]
        tile = 256
        for r in range(0, n_rows, tile):
            sl = slice(r, r + tile)
            y_ref[sl, :] = _rms(y_ref[sl, :] + ob_ref[sl, :], ng) * _silu(z_ref[sl, :])

    finish(yc_ref, obc_ref, zc_ref)
    finish(yl_ref, obl_ref, zl_ref)


def _deltanet(pl_f32, pc_f32, conv_w, alog_rows, dtb_rows, ng):
    b, t, _ = pl_f32.shape
    tc = pc_f32.shape[1]
    h = DN_HEADS
    n_tot = (t + tc) // DN_CHUNK

    def specs(n):
        return [pl.BlockSpec((None, n, LANES), lambda bi, hi: (bi, 0, hi)),
                pl.BlockSpec((None, n, LANES), lambda bi, hi: (bi, 0, h + hi)),
                pl.BlockSpec((None, n, LANES), lambda bi, hi: (bi, 0, 2 * h + hi)),
                pl.BlockSpec((None, n, LANES), lambda bi, hi: (bi, 0, 3 * h + hi)),
                pl.BlockSpec((None, n, LANES), lambda bi, hi: (bi, 0, 4 * h + hi))]
    in_specs = specs(t) + specs(tc) + [
        pl.BlockSpec((3, LANES), lambda bi, hi: (0, hi)),
        pl.BlockSpec((3, LANES), lambda bi, hi: (0, h + hi)),
        pl.BlockSpec((3, LANES), lambda bi, hi: (0, 2 * h + hi)),
        pl.BlockSpec((None, 1, LANES), lambda bi, hi: (hi, 0, 0)),
        pl.BlockSpec((None, 1, LANES), lambda bi, hi: (hi, 0, 0)),
        pl.BlockSpec((1, LANES), lambda bi, hi: (0, 0))]
    yl, yc = pl.pallas_call(
        functools.partial(_dn_kernel, n_lat=t // DN_CHUNK, n_ctx=tc // DN_CHUNK),
        grid=(b, h),
        in_specs=in_specs,
        out_specs=[pl.BlockSpec((None, t, LANES), lambda bi, hi: (bi, 0, hi)),
                   pl.BlockSpec((None, tc, LANES), lambda bi, hi: (bi, 0, hi))],
        out_shape=[jax.ShapeDtypeStruct((b, t, h * LANES), F32),
                   jax.ShapeDtypeStruct((b, tc, h * LANES), F32)],
        scratch_shapes=[pltpu.VMEM((t, LANES), F32)] * 3 + [pltpu.VMEM((tc, LANES), F32)] * 3
                       + [pltpu.VMEM((t, LANES), F32), pltpu.VMEM((tc, LANES), F32),
                          pltpu.VMEM((2, n_tot, LANES, LANES), BF16), pltpu.VMEM((2, n_tot, LANES, LANES), F32),
                          pltpu.VMEM((2, n_tot, DN_CHUNK, LANES), BF16), pltpu.VMEM((2, n_tot, DN_CHUNK, LANES), F32),
                          pltpu.VMEM((2, n_tot, 8, LANES), F32)],
        compiler_params=_params("arbitrary", "arbitrary"),
        name="deltanet",
    )(*([pl_f32] * 5 + [pc_f32] * 5 + [conv_w, conv_w, conv_w, alog_rows, dtb_rows, ng.reshape(1, LANES)]))
    return yl, yc


_FF_CHUNKS = ((0, 1536), (1536, D_FF))


def _post_kernel(*refs, final):
    if final:
        x_ref, ya_ref, yb_ref, mod_ref, gf_ref, wa_ref, wb_ref, wg_ref, wu_ref, wd_ref, gfin_ref, o_ref = refs
    else:
        x_ref, ya_ref, yb_ref, mod_ref, gf_ref, wa_ref, wb_ref, wg_ref, wu_ref, wd_ref, o_ref = refs
    mod = mod_ref[...]
    ga1 = mod[:, 2 * D_MODEL:3 * D_MODEL]
    ga2 = mod[:, 5 * D_MODEL:6 * D_MODEL]
    y = _dot(_bf(ya_ref[...]), wa_ref[...]) + _dot(_bf(yb_ref[...]), wb_ref[...])
    x1 = x_ref[...] + ga1 * y
    hb = _bf(_norm_mod(x1, gf_ref[...], mod, 3))
    acc = None
    for lo, hi in _FF_CHUNKS:
        gate = _dot(hb, wg_ref[:, lo:hi])
        up = _dot(hb, wu_ref[:, lo:hi])
        part = _dot(_bf(_silu(gate) * up), wd_ref[lo:hi, :])
        acc = part if acc is None else acc + part
    out = x1 + ga2 * acc
    if final:
        out = _rms(out, gfin_ref[...])
    o_ref[...] = out


def _post(x, ya, yb, mods, gf, wa, wb, wg, wu, wd, gfin, *, mod_row, tm):
    b, t, _ = x.shape
    rows = b * t
    nb = rows // tm
    final = gfin is not None
    na, nbw = ya.shape[-1], yb.shape[-1]

    def const(shape):
        return pl.BlockSpec(shape, lambda i: (0, 0), pipeline_mode=pl.Buffered(1))
    in_specs = [pl.BlockSpec((tm, D_MODEL), lambda i: (i, 0)),
                pl.BlockSpec((tm, na), lambda i: (i, 0)),
                pl.BlockSpec((tm, nbw), lambda i: (i, 0)),
                pl.BlockSpec((None, 1, 6 * D_MODEL), lambda i: (mod_row(i), 0, 0)),
                pl.BlockSpec((1, D_MODEL), lambda i: (0, 0)),
                const((na, D_MODEL)), const((nbw, D_MODEL)),
                const((D_MODEL, D_FF)), const((D_MODEL, D_FF)), const((D_FF, D_MODEL))]
    args = [x.reshape(rows, D_MODEL), ya.reshape(rows, na), yb.reshape(rows, nbw), mods,
            gf.reshape(1, D_MODEL), wa, wb, wg, wu, wd]
    if final:
        in_specs.append(pl.BlockSpec((1, D_MODEL), lambda i: (0, 0)))
        args.append(gfin.reshape(1, D_MODEL))
    out = pl.pallas_call(
        functools.partial(_post_kernel, final=final),
        grid=(nb,),
        in_specs=in_specs,
        out_specs=pl.BlockSpec((tm, D_MODEL), lambda i: (i, 0)),
        out_shape=jax.ShapeDtypeStruct((rows, D_MODEL), F32),
        compiler_params=_params("arbitrary"),
        name="post_ffn",
    )(*args)
    return out.reshape(b, t, D_MODEL)


def _rope_tables(n_tok):
    rows = n_tok // GRID_W
    quarter = HD // 4
    freqs = ROPE_THETA ** (-jnp.arange(quarter, dtype=F32) / quarter)
    row = jnp.repeat(jnp.arange(rows, dtype=F32), GRID_W)
    col = jnp.tile(jnp.arange(GRID_W, dtype=F32), rows)
    ang = jnp.concatenate([row[:, None] * freqs, col[:, None] * freqs], axis=-1)
    cos, sin = jnp.cos(ang), jnp.sin(ang)
    lane = np.arange(LANES)
    d = lane % HD
    src = (d // 32) * quarter + (d % quarter)
    first = (d % 32) < quarter
    c = cos[:, src]
    s = sin[:, src]
    s1 = jnp.where(first[None, :], -s, 0.0)
    s2 = jnp.where(first[None, :], 0.0, s)
    return c, s1, s2


def _place(cols_src, width, dst_lanes):
    idx = np.zeros((width,), np.int32)
    msk = np.zeros((width,), bool)
    for src, dst in zip(cols_src, dst_lanes):
        idx[dst] = src
        msk[dst] = True
    return idx, msk


def _gather_cols(w, idx, msk):
    return jnp.where(jnp.asarray(msk)[None, :], w[:, jnp.asarray(idx)], 0.0)


def _even_layout():
    src, dst = [], []
    for i in range(3072):
        src.append(i)
        dst.append(i)
    gbase = 3072
    for p in range(2):
        for kind in range(4):
            for hh in range(2):
                src.append(gbase + kind * ML_HEADS + 2 * p + hh)
                dst.append(3072 + p * LANES + kind * 2 + hh)
    return _place(src, EVEN_BF + EVEN_F32, dst)


def _odd_layout():
    src, dst = [], []
    per_kv = GQ_HEADS // GQ_KV
    for h in range(GQ_HEADS):
        j = h // per_kv
        for e in range(HD):
            src.append(h * HD + e)
            dst.append(h * LANES + j * HD + e)
    for j in range(GQ_KV):
        for e in range(HD):
            src.append(512 + j * HD + e)
            dst.append(GQ_HEADS * LANES + j * LANES + j * HD + e)
    for e in range(GQ_KV * HD):
        src.append(640 + e)
        dst.append((GQ_HEADS + GQ_KV) * LANES + e)
    for e in range(1536 + 512):
        src.append(768 + e)
        dst.append(ODD_BF + e)
    for h in range(DN_HEADS):
        for d in range(2):
            for kind in range(2):
                src.append(2816 + d * 2 * DN_HEADS + kind * DN_HEADS + h)
                dst.append(ODD_BF + 2048 + h * LANES + d * 2 + kind)
    return _place(src, ODD_BF + ODD_F32, dst)


def kernel(x, c, ctx, c_ctx, w_ada, b_ada, g_mix, g_ffn, w_in_e, b_gate_e, da_lam, da_norm_g, ml_norm_g,
           w_out_e, w_in_o, qk_norm_g, dn_conv, dn_a_log, dn_dt_bias, dn_norm_g, w_out_o, w_gate, w_up,
           w_down, g_final):
    b, t, _ = x.shape
    tc = ctx.shape[1]
    tables = _rope_tables(t)

    cv = jnp.zeros((16, D_MODEL), F32).at[:b].set(c).at[b].set(c_ctx)
    mods = _ada_mods(cv, w_ada, b_ada).reshape(DEPTH, 16, 1, 6 * D_MODEL)

    e_idx, e_msk = _even_layout()
    o_idx, o_msk = _odd_layout()
    tm_l = 512
    tm_c = 256
    bpb = t // tm_l
    lat_row = lambda i: i // bpb
    ctx_row = lambda i: b

    xl, xc = x, ctx
    for layer in range(DEPTH):
        emit_ctx = layer < DEPTH - 1
        m = mods[layer]
        if layer % 2 == 0:
            e = layer // 2
            lam_init = 0.8 - 0.6 * math.exp(-0.3 * layer)
            w = _bf(_gather_cols(w_in_e[e], e_idx, e_msk))
            pl_bf, pl_f = _inproj(xl, m, g_mix[layer], w, [], tables, even=True, seq_len=t, mod_row=lat_row, tm=tm_l)
            pc_bf, pc_f = _inproj(xc, m, g_mix[layer], w, [], None, even=True, seq_len=tc, mod_row=ctx_row, tm=tm_c)
            ya_l = _diff_attn(pl_bf, [pc_bf, pl_bf], da_lam[e], da_norm_g[e], lam_init, 256, True)
            gbias = jnp.zeros((2, 1, LANES), F32)
            gb = b_gate_e[e].reshape(4, 2, 2)
            gbias = gbias.at[:, 0, :8].set(gb.transpose(1, 0, 2).reshape(2, 8))
            yb_l, yb_c = _mlstm(pl_f, pc_f, gbias, ml_norm_g[e])
            if emit_ctx:
                ya_c = _diff_attn(pc_bf, [pc_bf], da_lam[e], da_norm_g[e], lam_init, tc, False)
            w_out = _bf(w_out_e[e])
            wa, wb = w_out[:DA_HEADS * LANES], w_out[DA_HEADS * LANES:]
        else:
            o = layer // 2
            w = _bf(_gather_cols(w_in_o[o], o_idx, o_msk))
            qkg = jnp.zeros((2 * GQ_KV, LANES), F32)
            for j in range(GQ_KV):
                qkg = qkg.at[j, j * HD:(j + 1) * HD].set(qk_norm_g[o, 0])
                qkg = qkg.at[GQ_KV + j, j * HD:(j + 1) * HD].set(qk_norm_g[o, 1])
            pl_bf, pl_f = _inproj(xl, m, g_mix[layer], w, [qkg], tables, even=False, seq_len=t, mod_row=lat_row, tm=tm_l)
            pc_bf, pc_f = _inproj(xc, m, g_mix[layer], w, [qkg], None, even=False, seq_len=tc, mod_row=ctx_row, tm=tm_c)
            ya_l = _gqa_attn(pl_bf, [pc_bf, pl_bf], 256, True)
            alog = jnp.zeros((DN_HEADS, 1, LANES), F32)
            dtb = jnp.zeros((DN_HEADS, 1, LANES), F32)
            for d in range(2):
                alog = alog.at[:, 0, 2 * d + 1].set(dn_a_log[o, d])
                dtb = dtb.at[:, 0, 2 * d + 1].set(dn_dt_bias[o, d])
            yb_l, yb_c = _deltanet(pl_f, pc_f, dn_conv[o], alog, dtb, dn_norm_g[o])
            if emit_ctx:
                ya_c = _gqa_attn(pc_bf, [pc_bf], tc, False)
            w_out = _bf(w_out_o[o])
            wa, wb = w_out[:GQ_HEADS * HD], w_out[GQ_HEADS * HD:]
        wg, wu, wd = _bf(w_gate[layer]), _bf(w_up[layer]), _bf(w_down[layer])
        gfin = g_final if layer == DEPTH - 1 else None
        xl = _post(xl, ya_l, yb_l, m, g_ffn[layer], wa, wb, wg, wu, wd, gfin, mod_row=lat_row, tm=tm_l)
        if emit_ctx:
            xc = _post(xc, ya_c, yb_c, m, g_ffn[layer], wa, wb, wg, wu, wd, None, mod_row=ctx_row, tm=tm_c)
    return xl
```

```python
import functools
import math

import jax
import jax.numpy as jnp
import numpy as np
from jax import lax
from jax.experimental import pallas as pl
from jax.experimental.pallas import tpu as pltpu

F32 = jnp.float32
BF16 = jnp.bfloat16

D_MODEL = 1024
DEPTH = 4
GRID_W = 64
ROPE_THETA = 10000.0
EPS = 1e-6
HD = 64
Q_BLOCK = 128
LANES = 128
DA_HEADS = 4
ML_HEADS = 4
GQ_HEADS = 8
GQ_KV = 2
DN_HEADS = 4
D_FF = 2816
ML_CHUNK = 128
DN_CHUNK = 64
DN_GROUP = 8
VMEM_LIMIT = 56 * 1024 * 1024

EVEN_BF = 1536
EVEN_F32 = 1792
ODD_BF = 1408
ODD_F32 = 2560


def _bf(x):
    return x.astype(BF16)


def _dot(a, b):
    return jnp.dot(a, b, preferred_element_type=F32)


def _dot_nt(a, b):
    return lax.dot_general(a, b, (((1,), (1,)), ((), ())), preferred_element_type=F32)


def _dot_tn(a, b):
    return lax.dot_general(a, b, (((0,), (0,)), ((), ())), preferred_element_type=F32)


def _split2(x):
    hi = _bf(x)
    return hi, _bf(x - hi.astype(F32))


def _split3(x):
    hi = _bf(x)
    r = x - hi.astype(F32)
    mid = _bf(r)
    return hi, mid, _bf(r - mid.astype(F32))


def _dot_x3(a, b):
    ah, al = _split2(a)
    bh, bl = _split2(b)
    return _dot(ah, bh) + (_dot(ah, bl) + _dot(al, bh))


def _sel_dot(sel, x):
    h, m, l = _split3(x)
    return _dot(sel, h) + (_dot(sel, m) + _dot(sel, l))


def _sel_dot_nt(sel, x):
    h, m, l = _split3(x)
    return _dot_nt(sel, h) + (_dot_nt(sel, m) + _dot_nt(sel, l))


def _sigmoid(x):
    return 1.0 / (1.0 + jnp.exp(-x))


def _silu(x):
    return x * _sigmoid(x)


def _log_sigmoid(x):
    return jnp.minimum(x, 0.0) - jnp.log(1.0 + jnp.exp(-jnp.abs(x)))


def _softplus(x):
    return jnp.maximum(x, 0.0) + jnp.log(1.0 + jnp.exp(-jnp.abs(x)))


def _rms(x, g, n=None):
    n = x.shape[-1] if n is None else n
    ss = jnp.sum(x * x, axis=-1, keepdims=True)
    return x * lax.rsqrt(ss * (1.0 / n) + EPS) * g


def _norm_mod(x, g, mod, slot):
    sh = mod[:, slot * D_MODEL:(slot + 1) * D_MODEL]
    sc = mod[:, (slot + 1) * D_MODEL:(slot + 2) * D_MODEL]
    return _rms(x, g) * (1.0 + sc) + sh


def _rope(x, c, s1, s2):
    return x * c + pltpu.roll(x, LANES - 16, 1) * s1 + pltpu.roll(x, 16, 1) * s2


def _params(*sem):
    return pltpu.CompilerParams(dimension_semantics=sem, vmem_limit_bytes=VMEM_LIMIT)


def _ada_kernel(cv_ref, w_ref, b_ref, o_ref):
    cv = cv_ref[...]
    o_ref[...] = _dot_x3(_silu(cv), w_ref[...]) + b_ref[...]


def _ada_mods(cv, w_ada, b_ada):
    rows = cv.shape[0]
    nb = 6
    return pl.pallas_call(
        _ada_kernel,
        grid=(DEPTH, nb),
        in_specs=[pl.BlockSpec((rows, D_MODEL), lambda l, j: (0, 0)),
                  pl.BlockSpec((None, D_MODEL, D_MODEL), lambda l, j: (l, 0, j)),
                  pl.BlockSpec((None, 1, D_MODEL), lambda l, j: (l, 0, j))],
        out_specs=pl.BlockSpec((None, rows, D_MODEL), lambda l, j: (l, 0, j)),
        out_shape=jax.ShapeDtypeStruct((DEPTH, rows, 6 * D_MODEL), F32),
        compiler_params=_params("arbitrary", "arbitrary"),
        name="ada_mods",
    )(cv, w_ada, b_ada.reshape(DEPTH, 1, 6 * D_MODEL))


def _inproj_even_kernel(*refs, rope):
    if rope:
        x_ref, mod_ref, g_ref, w_ref, rc_ref, s1_ref, s2_ref, o1_ref, o2_ref = refs
    else:
        x_ref, mod_ref, g_ref, w_ref, o1_ref, o2_ref = refs
    hb = _bf(_norm_mod(x_ref[...], g_ref[...], mod_ref[...], 0))
    for seg in range(4):
        p = _dot(hb, w_ref[:, seg * 256:(seg + 1) * 256])
        for half in range(2):
            gi = seg * 2 + half
            xg = p[:, half * LANES:(half + 1) * LANES]
            if rope:
                xg = _rope(xg, rc_ref[...], s1_ref[...], s2_ref[...])
            if gi < DA_HEADS:
                xg = xg * (HD ** -0.5)
            o1_ref[:, gi * LANES:(gi + 1) * LANES] = _bf(xg)
    o1_ref[:, 1024:EVEN_BF] = _bf(_dot(hb, w_ref[:, 1024:EVEN_BF]))
    o2_ref[...] = _dot(hb, w_ref[:, EVEN_BF:])


def _inproj_odd_kernel(*refs, rope):
    if rope:
        x_ref, mod_ref, g_ref, w_ref, qkg_ref, rc_ref, s1_ref, s2_ref, o1_ref, o2_ref = refs
    else:
        x_ref, mod_ref, g_ref, w_ref, qkg_ref, o1_ref, o2_ref = refs
    hb = _bf(_norm_mod(x_ref[...], g_ref[...], mod_ref[...], 0))
    for seg in range(5):
        p = _dot(hb, w_ref[:, seg * 256:(seg + 1) * 256])
        for half in range(2):
            gi = seg * 2 + half
            xg = p[:, half * LANES:(half + 1) * LANES]
            if gi < GQ_HEADS:
                grow = gi // (GQ_HEADS // GQ_KV)
            else:
                grow = GQ_KV + (gi - GQ_HEADS)
            xg = _rms(xg, qkg_ref[grow:grow + 1, :], HD)
            if rope:
                xg = _rope(xg, rc_ref[...], s1_ref[...], s2_ref[...])
            if gi < GQ_HEADS:
                xg = xg * (HD ** -0.5)
            o1_ref[:, gi * LANES:(gi + 1) * LANES] = _bf(xg)
    o1_ref[:, 1280:ODD_BF] = _bf(_dot(hb, w_ref[:, 1280:ODD_BF]))
    o2_ref[...] = _dot(hb, w_ref[:, ODD_BF:])


def _inproj(x, mods, g, w, extra, tables, *, even, seq_len, mod_row, tm):
    b, t, _ = x.shape
    rows = b * t
    nb = rows // tm
    bpb = max(t // tm, 1)
    n_bf, n_f32 = (EVEN_BF, EVEN_F32) if even else (ODD_BF, ODD_F32)
    rope = tables is not None
    kern = functools.partial(_inproj_even_kernel if even else _inproj_odd_kernel, rope=rope)
    in_specs = [pl.BlockSpec((tm, D_MODEL), lambda i: (i, 0)),
                pl.BlockSpec((None, 1, 6 * D_MODEL), lambda i: (mod_row(i), 0, 0)),
                pl.BlockSpec((1, D_MODEL), lambda i: (0, 0)),
                pl.BlockSpec((D_MODEL, n_bf + n_f32), lambda i: (0, 0))]
    args = [x.reshape(rows, D_MODEL), mods, g.reshape(1, D_MODEL), w]
    for e in extra:
        in_specs.append(pl.BlockSpec(e.shape, lambda i: (0, 0)))
        args.append(e)
    if rope:
        for tab in tables:
            in_specs.append(pl.BlockSpec((tm, LANES), lambda i: (i % bpb, 0)))
            args.append(tab)
    o1, o2 = pl.pallas_call(
        kern,
        grid=(nb,),
        in_specs=in_specs,
        out_specs=[pl.BlockSpec((tm, n_bf), lambda i: (i, 0)),
                   pl.BlockSpec((tm, n_f32), lambda i: (i, 0))],
        out_shape=[jax.ShapeDtypeStruct((rows, n_bf), BF16),
                   jax.ShapeDtypeStruct((rows, n_f32), F32)],
        compiler_params=_params("arbitrary"),
        name="inproj_even" if even else "inproj_odd",
    )(*args)
    return o1.reshape(b, t, n_bf), o2.reshape(b, t, n_f32)


def _with_ones(v):
    return jnp.concatenate([v, jnp.ones_like(v)], axis=1)


def _softmax_pv(q, kvs):
    ss = [_dot_nt(q, k) for k, _ in kvs]
    mx = ss[0].max(axis=-1, keepdims=True)
    for s in ss[1:]:
        mx = jnp.maximum(mx, s.max(axis=-1, keepdims=True))
    acc = None
    for s, (_, v) in zip(ss, kvs):
        o = _dot(_bf(jnp.exp(s - mx)), v)
        acc = o if acc is None else acc + o
    return acc[:, :LANES] / acc[:, LANES:]


def _store_heads(o_ref, o, group, width, blocked):
    if not blocked:
        o_ref[:, group * LANES:(group + 1) * LANES] = o
        return
    for jj in range(o.shape[0] // Q_BLOCK):
        col = jj * width + group * LANES
        o_ref[:, col:col + LANES] = o[jj * Q_BLOCK:(jj + 1) * Q_BLOCK, :]


def _attn_out(b, t, width, tq, blocked):
    if blocked:
        nblk = t // Q_BLOCK
        spec = pl.BlockSpec((None, Q_BLOCK, (tq // Q_BLOCK) * width), lambda bi, qi: (bi, 0, qi))
        return spec, jax.ShapeDtypeStruct((b, Q_BLOCK, nblk * width), F32)
    return (pl.BlockSpec((None, tq, width), lambda bi, qi: (bi, qi, 0)),
            jax.ShapeDtypeStruct((b, t, width), F32))


def _diff_attn_kernel(*refs, n_kv, lam_init, blocked):
    q_ref = refs[0]
    kv_refs = refs[1:1 + 2 * n_kv]
    lam_ref, ng_ref, o_ref = refs[1 + 2 * n_kv:]
    lane = lax.broadcasted_iota(jnp.int32, (1, LANES), 1)
    lf = lam_ref[...]
    s1 = jnp.sum(lf[0:1, :] * lf[1:2, :], axis=-1, keepdims=True)
    s2 = jnp.sum(lf[2:3, :] * lf[3:4, :], axis=-1, keepdims=True)
    lmb = jnp.exp(s1) - jnp.exp(s2) + lam_init
    ng = ng_ref[...]
    for h in range(DA_HEADS):
        hs = slice(h * LANES, (h + 1) * LANES)
        q = q_ref[:, hs]
        kvs = [(kv_refs[2 * i][:, hs], _with_ones(kv_refs[2 * i + 1][:, hs])) for i in range(n_kv)]
        outs = []
        for m in range(2):
            in_map = (lane >= m * HD) & (lane < (m + 1) * HD)
            qm = jnp.where(in_map, q, jnp.zeros_like(q))
            outs.append(_softmax_pv(qm, kvs))
        o = outs[0] - lmb * outs[1]
        _store_heads(o_ref, _rms(o, ng) * (1.0 - lam_init), h, DA_HEADS * LANES, blocked)


def _diff_attn(q_src, kv_srcs, lam, ng, lam_init, tq, blocked):
    b, t, _ = q_src.shape
    nq = t // tq
    width = DA_HEADS * LANES
    in_specs = [pl.BlockSpec((None, tq, width), lambda bi, qi: (bi, qi, 0))]
    args = [q_src]
    for src in kv_srcs:
        n = src.shape[1]
        in_specs.append(pl.BlockSpec((None, n, width), lambda bi, qi: (bi, 0, 1)))
        in_specs.append(pl.BlockSpec((None, n, width), lambda bi, qi: (bi, 0, 2)))
        args += [src, src]
    in_specs += [pl.BlockSpec((4, HD), lambda bi, qi: (0, 0)),
                 pl.BlockSpec((1, LANES), lambda bi, qi: (0, 0))]
    args += [lam, ng.reshape(1, LANES)]
    out_spec, out_shape = _attn_out(b, t, width, tq, blocked)
    out = pl.pallas_call(
        functools.partial(_diff_attn_kernel, n_kv=len(kv_srcs), lam_init=lam_init, blocked=blocked),
        grid=(b, nq),
        in_specs=in_specs,
        out_specs=out_spec,
        out_shape=out_shape,
        compiler_params=_params("arbitrary", "arbitrary"),
        name="diff_attn",
    )(*args)
    return out.reshape(b, t, width)


def _gqa_kernel(*refs, n_kv, blocked):
    q_ref = refs[0]
    kv_refs = refs[1:1 + 2 * n_kv]
    o_ref = refs[1 + 2 * n_kv]
    lane = lax.broadcasted_iota(jnp.int32, (1, LANES), 1)
    low = lane < HD
    per_kv = GQ_HEADS // GQ_KV
    vals = [_with_ones(kv_refs[2 * i + 1][...]) for i in range(n_kv)]
    for pair in range(GQ_HEADS // 2):
        j = (2 * pair) // per_kv
        kvs = [(kv_refs[2 * i][:, j * LANES:(j + 1) * LANES], vals[i]) for i in range(n_kv)]
        o_a = _softmax_pv(q_ref[:, (2 * pair) * LANES:(2 * pair + 1) * LANES], kvs)
        o_b = _softmax_pv(q_ref[:, (2 * pair + 1) * LANES:(2 * pair + 2) * LANES], kvs)
        if j == 0:
            packed = jnp.where(low, o_a, pltpu.roll(o_b, HD, 1))
        else:
            packed = jnp.where(low, pltpu.roll(o_a, HD, 1), o_b)
        _store_heads(o_ref, packed, pair, GQ_HEADS * HD, blocked)


def _gqa_attn(q_src, kv_srcs, tq, blocked):
    b, t, _ = q_src.shape
    nq = t // tq
    qw = GQ_HEADS * LANES
    in_specs = [pl.BlockSpec((None, tq, qw), lambda bi, qi: (bi, qi, 0))]
    args = [q_src]
    for src in kv_srcs:
        n = src.shape[1]
        in_specs.append(pl.BlockSpec((None, n, GQ_KV * LANES), lambda bi, qi: (bi, 0, qw // (GQ_KV * LANES))))
        in_specs.append(pl.BlockSpec((None, n, LANES), lambda bi, qi: (bi, 0, (qw + GQ_KV * LANES) // LANES)))
        args += [src, src]
    out_spec, out_shape = _attn_out(b, t, GQ_HEADS * HD, tq, blocked)
    out = pl.pallas_call(
        functools.partial(_gqa_kernel, n_kv=len(kv_srcs), blocked=blocked),
        grid=(b, nq),
        in_specs=in_specs,
        out_specs=out_spec,
        out_shape=out_shape,
        compiler_params=_params("arbitrary", "arbitrary"),
        name="gqa_attn",
    )(*args)
    return out.reshape(b, t, GQ_HEADS * HD)


def _mlstm_kernel(ql_ref, kl_ref, vl_ref, ol_ref, gl_ref, qc_ref, kc_ref, vc_ref, oc_ref, gc_ref,
                  gb_ref, ng_ref, yl_ref, yc_ref, hbl_ref, hbc_ref, cst_ref, mst_ref, *, n_lat, n_ctx):
    c = ML_CHUNK
    lane = lax.broadcasted_iota(jnp.int32, (1, LANES), 1)
    ri = lax.broadcasted_iota(jnp.int32, (c, c), 0)
    ci = lax.broadcasted_iota(jnp.int32, (c, c), 1)
    masks = (ci <= ri, ci >= ri)
    cum_mat = _bf(masks[0].astype(F32))
    row_id = lax.broadcasted_iota(jnp.int32, (c, LANES), 0)
    ones_blk = jnp.ones((c, LANES), F32)
    gb = gb_ref[...]
    k3 = lax.broadcasted_iota(jnp.int32, (3 * LANES, LANES), 0) & (LANES - 1)
    c3 = lax.broadcasted_iota(jnp.int32, (c, 3 * LANES), 1) & (LANES - 1)
    units = [(d, hh) for d in range(2) for hh in range(2)]
    gate_lane = [d * 4 + hh for d, hh in units]
    sel_t = [_bf((k3 == cl).astype(F32)) for cl in gate_lane]
    sel_r = [_bf((c3 == cl).astype(F32)) for cl in gate_lane]
    head_mask = [(lane >= hh * HD) & (lane < (hh + 1) * HD) for hh in range(2)]

    cst_ref[...] = jnp.zeros_like(cst_ref)
    mst_ref[...] = jnp.zeros_like(mst_ref)

    def cummax_rows(x, reverse):
        k = 1
        while k < c:
            if reverse:
                sh = jnp.where(row_id < c - k, pltpu.roll(x, c - k, 0), -jnp.inf)
            else:
                sh = jnp.where(row_id >= k, pltpu.roll(x, k, 0), -jnp.inf)
            x = jnp.maximum(x, sh)
            k *= 2
        return x

    def gate_prep(g_ref, r0, d):
        g = g_ref[pl.ds(r0, c), :] + gb
        lf = _log_sigmoid(g)
        cum = _sel_dot(cum_mat, lf)
        if d == 1:
            cum = jnp.sum(lf, axis=0, keepdims=True) - cum + lf
        bc = pltpu.roll(cum, LANES - 2, 1)
        w = g - bc
        cmx = cummax_rows(w, d == 1)
        edge = slice(c - 1, c) if d == 0 else slice(0, 1)
        h, m, l = _split3(jnp.concatenate([w, cmx, bc], axis=0))
        return jnp.concatenate([h, m, l], axis=1), bc[edge, :], cmx[edge, :]

    def chunk_step(q_ref, k_ref, v_ref, g_ref, dst_refs, rows):
        preps = [gate_prep(g_ref, rows[d], d) for d in range(2)]
        q2 = [q_ref[pl.ds(rows[d], c), :] for d in range(2)]
        k2 = [k_ref[pl.ds(rows[d], c), :] * (HD ** -0.5) for d in range(2)]
        v2 = [v_ref[pl.ds(rows[d], c), :] for d in range(2)]
        qm = [_bf(jnp.where(head_mask[hh], q2[d], 0.0)) for d, hh in units]
        km = [_bf(jnp.where(head_mask[hh], k2[d], 0.0)) for d, hh in units]
        vh = [v2[d][:, hh * LANES:(hh + 1) * LANES] for d, hh in units]
        bb = [_dot(preps[d][0], sel_t[i]) for i, (d, hh) in enumerate(units)]
        w_b = [x[0:c] for x in bb]
        cm_b = [x[c:2 * c] for x in bb]
        bc_b = [x[2 * c:3 * c] for x in bb]
        w_row = [_dot_nt(sel_r[i], preps[d][0][0:c]) for i, (d, hh) in enumerate(units)]
        qk = _each(_dot_nt, qm, km)
        p = [jnp.exp(jnp.where(masks[d], wr - cm, -jnp.inf)) * s
             for (d, hh), wr, cm, s in zip(units, w_row, cm_b, qk)]
        num_loc = [_dot(_bf(pi), _bf(jnp.concatenate([v, ones_blk], axis=1))) for pi, v in zip(p, vh)]

        m_st = [mst_ref[i][0:1, 0:1] for i in range(4)]
        c_st = [cst_ref[i] for i in range(4)]
        b_end = [preps[d][1][:, cl:cl + 1] for (d, hh), cl in zip(units, gate_lane)]
        w_max = [preps[d][2][:, cl:cl + 1] for (d, hh), cl in zip(units, gate_lane)]
        m_new = [jnp.maximum(be + ms, be + wm) for be, ms, wm in zip(b_end, m_st, w_max)]
        top = _each(jnp.maximum, m_st, cm_b)
        inter = [jnp.exp(ms - t) for ms, t in zip(m_st, top)]
        s_loc = [jnp.exp(cm - t) for cm, t in zip(cm_b, top)]
        floor = [jnp.exp(-(bc + t)) for bc, t in zip(bc_b, top)]
        q_c = [_dot(q, _bf(cs)) for q, cs in zip(qm, c_st)]
        for i, (d, hh) in enumerate(units):
            num = inter[i] * q_c[i][:, :LANES] + s_loc[i] * num_loc[i][:, :LANES]
            den = inter[i] * q_c[i][:, LANES:] + s_loc[i] * num_loc[i][:, LANES:]
            dst_refs[d][pl.ds(rows[d], c), hh * LANES:(hh + 1) * LANES] = num / jnp.maximum(jnp.abs(den), floor[i])
        w_in = [jnp.exp(be + wb - mn) for be, wb, mn in zip(b_end, w_b, m_new)]
        kv = [_dot_tn(k, _bf(jnp.concatenate([w * v, w], axis=1))) for k, w, v in zip(km, w_in, vh)]
        for i in range(4):
            cst_ref[i] = jnp.exp(b_end[i] + m_st[i] - m_new[i]) * c_st[i] + kv[i]
            mst_ref[i] = jnp.broadcast_to(m_new[i], mst_ref.shape[1:])

    for cc in range(n_ctx):
        chunk_step(qc_ref, kc_ref, vc_ref, gc_ref, (yc_ref, hbc_ref), (cc * c, (n_ctx - 1 - cc) * c))

    def body(cc, carry):
        rows = (pl.multiple_of(cc * c, c), pl.multiple_of((n_lat - 1 - cc) * c, c))
        chunk_step(ql_ref, kl_ref, vl_ref, gl_ref, (yl_ref, hbl_ref), rows)
        return carry

    lax.fori_loop(0, n_lat, body, 0)

    ng = ng_ref[...]

    def finish(y_ref, hb_ref, o_ref, n_rows):
        tile = 256
        for r in range(0, n_rows, tile):
            for hh in range(2):
                sl = (slice(r, r + tile), slice(hh * LANES, (hh + 1) * LANES))
                hsum = y_ref[sl] + hb_ref[sl]
                y_ref[sl] = _rms(hsum, ng) * _sigmoid(o_ref[sl])

    finish(yc_ref, hbc_ref, oc_ref, n_ctx * c)
    finish(yl_ref, hbl_ref, ol_ref, n_lat * c)


def _mlstm(pl_f32, pc_f32, gate_bias, ng):
    b, t, _ = pl_f32.shape
    tc = pc_f32.shape[1]
    n_pairs = ML_HEADS // 2
    def specs(n):
        return [pl.BlockSpec((None, n, LANES), lambda bi, p: (bi, 0, p)),
                pl.BlockSpec((None, n, LANES), lambda bi, p: (bi, 0, 2 + p)),
                pl.BlockSpec((None, n, 2 * LANES), lambda bi, p: (bi, 0, 2 + p)),
                pl.BlockSpec((None, n, 2 * LANES), lambda bi, p: (bi, 0, 4 + p)),
                pl.BlockSpec((None, n, LANES), lambda bi, p: (bi, 0, 12 + p))]
    in_specs = specs(t) + specs(tc) + [pl.BlockSpec((None, 1, LANES), lambda bi, p: (p, 0, 0)),
                                      pl.BlockSpec((1, LANES), lambda bi, p: (0, 0))]
    yl, yc = pl.pallas_call(
        functools.partial(_mlstm_kernel, n_lat=t // ML_CHUNK, n_ctx=tc // ML_CHUNK),
        grid=(b, n_pairs),
        in_specs=in_specs,
        out_specs=[pl.BlockSpec((None, t, 2 * LANES), lambda bi, p: (bi, 0, p)),
                   pl.BlockSpec((None, tc, 2 * LANES), lambda bi, p: (bi, 0, p))],
        out_shape=[jax.ShapeDtypeStruct((b, t, ML_HEADS * LANES), F32),
                   jax.ShapeDtypeStruct((b, tc, ML_HEADS * LANES), F32)],
        scratch_shapes=[pltpu.VMEM((t, 2 * LANES), F32), pltpu.VMEM((tc, 2 * LANES), F32),
                        pltpu.VMEM((4, LANES, 2 * LANES), F32), pltpu.VMEM((4, 8, LANES), F32)],
        compiler_params=_params("arbitrary", "arbitrary"),
        name="mlstm",
    )(*([pl_f32] * 5 + [pc_f32] * 5 + [gate_bias, ng.reshape(1, LANES)]))
    return yl, yc


def _each(f, *lists):
    return [f(*xs) for xs in zip(*lists)]


def _unit_tri_inverse(a, eye, blk16, blk32):
    def mm(u, w):
        return _dot(_bf(u), _bf(w))

    d = [jnp.where(blk16, ai, 0.0) for ai in a]
    x = [eye - di for di in d]
    p = _each(mm, d, d)
    for step in range(3):
        x = _each(lambda xi, ti: xi + ti, x, _each(mm, x, p))
        if step < 2:
            p = _each(mm, p, p)
    e = [jnp.where(blk32 & jnp.logical_not(blk16), ai, 0.0) for ai in a]
    x = _each(lambda xi, ti: xi - ti, x, _each(mm, x, _each(mm, e, x)))
    e = [jnp.where(blk32, 0.0, ai) for ai in a]
    x = _each(lambda xi, ti: xi - ti, x, _each(mm, x, _each(mm, e, x)))
    return x


def _dn_kernel(ql_ref, kl_ref, vl_ref, zl_ref, gl_ref, qc_ref, kc_ref, vc_ref, zc_ref, gc_ref,
               wq_ref, wk_ref, wv_ref, alog_ref, dtb_ref, ng_ref, yl_ref, yc_ref,
               sql_ref, skl_ref, svl_ref, sqc_ref, skc_ref, svc_ref, obl_ref, obc_ref,
               p_ref, n_ref, qt_ref, o_ref, eg_ref, *, n_lat, n_ctx):
    c = DN_CHUNK
    lane = lax.broadcasted_iota(jnp.int32, (1, LANES), 1)
    ri = lax.broadcasted_iota(jnp.int32, (c, c), 0)
    ci = lax.broadcasted_iota(jnp.int32, (c, c), 1)
    incl = (ci <= ri, ci >= ri)
    strict = (ci < ri, ci > ri)
    cums = tuple(_bf(m.astype(F32)) for m in incl)
    eye = (ci == ri).astype(F32)
    blk16 = (ri >> 4) == (ci >> 4)
    blk32 = (ri >> 5) == (ci >> 5)
    neg_a = -jnp.exp(alog_ref[...])
    dtb = dtb_ref[...]

    def conv_prep(x_ref, w_ref, dst_ref, kind):
        n = x_ref.shape[0]
        x = x_ref[...]
        rows = lax.broadcasted_iota(jnp.int32, (n, LANES), 0)
        prev = jnp.where(rows == 0, 0.0, pltpu.roll(x, 1, 0))
        nxt = jnp.where(rows == n - 1, 0.0, pltpu.roll(x, n - 1, 0))
        w = w_ref[...]
        y = _silu(prev * w[0:1, :] + x * w[1:2, :] + nxt * w[2:3, :])
        if kind != "v":
            y = y * lax.rsqrt(jnp.sum(y * y, axis=-1, keepdims=True) + EPS)
        if kind == "q":
            y = y * (LANES ** -0.5)
        dst_ref[...] = y

    conv_prep(qc_ref, wq_ref, sqc_ref, "q")
    conv_prep(kc_ref, wk_ref, skc_ref, "k")
    conv_prep(vc_ref, wv_ref, svc_ref, "v")
    conv_prep(ql_ref, wq_ref, sql_ref, "q")
    conv_prep(kl_ref, wk_ref, skl_ref, "k")
    conv_prep(vl_ref, wv_ref, svl_ref, "v")
    sels = tuple(jnp.broadcast_to(_bf((lane == 2 * d + 1).astype(F32)), (c, LANES)) for d in range(2))

    def prep_group(q_ref, k_ref, v_ref, g_ref, c0, n_grp, n_seq, base):
        chunks = range(n_grp)
        units = [(u, d) for u in chunks for d in range(2)]
        rows = [pl.multiple_of((c0 + u) * c, c) for u in chunks]
        q = [q_ref[pl.ds(r, c), :] for r in rows]
        k = [k_ref[pl.ds(r, c), :] for r in rows]
        v = [v_ref[pl.ds(r, c), :] for r in rows]
        g = [g_ref[pl.ds(r, c), :] for r in rows]
        qb = [_bf(x) for x in q]
        kb = [_bf(x) for x in k]
        kk = _each(_dot_nt, kb, kb)
        qk = _each(_dot_nt, qb, kb)
        sig = [_sigmoid(x) for x in g]
        gdec = [neg_a * _softplus(x + dtb) for x in g]
        tot = [jnp.sum(x, axis=0, keepdims=True) for x in gdec]
        cum_f = [_sel_dot(cums[0], x) for x in gdec]
        cum = [jnp.where(lane < 2, cf, t - cf + x) for cf, t, x in zip(cum_f, tot, gdec)]
        gcum = [cum[u][:, 2 * d + 1:2 * d + 2] for u, d in units]
        g_end = [tot[u][:, 2 * d + 1:2 * d + 2] for u, d in units]
        beta = [sig[u][:, 2 * d:2 * d + 1] for u, d in units]
        g_row = [_sel_dot_nt(sels[d], cum[u]) for u, d in units]
        decay = [jnp.exp(jnp.where(incl[d], gc - gr, -jnp.inf)) for (u, d), gc, gr in zip(units, gcum, g_row)]
        a = [jnp.where(strict[d], bt * dc * kk[u], 0.0) for (u, d), bt, dc in zip(units, beta, decay)]
        tinv = _unit_tri_inverse(a, eye, blk16, blk32)
        gam = [jnp.exp(x) for x in gcum]
        rhs = [jnp.concatenate([bt * v[u], (bt * gm) * k[u]], axis=1) for (u, d), bt, gm in zip(units, beta, gam)]
        sol = [_bf(_dot(_bf(ti), _bf(r))) for ti, r in zip(tinv, rhs)]
        attn = [_bf(qk[u] * dc) for (u, d), dc in zip(units, decay)]
        kdec = [_bf(k[u] * jnp.exp(ge - gc)) for (u, d), ge, gc in zip(units, g_end, gcum)]
        qo = _each(_dot, attn, sol)
        pn = _each(_dot_tn, kdec, sol)
        for i, (u, d) in enumerate(units):
            chunk = c0 + u
            pos = base + (chunk if d == 0 else n_seq - 1 - chunk)
            p_ref[d, pos] = _bf(pn[i][:, LANES:])
            n_ref[d, pos] = pn[i][:, :LANES]
            qt_ref[d, pos] = _bf(gam[i] * q[u] - qo[i][:, LANES:])
            o_ref[d, pos] = qo[i][:, :LANES]
            eg_ref[d, pos] = jnp.broadcast_to(jnp.exp(g_end[i]), (8, LANES))

    grp_c = math.gcd(n_ctx, DN_GROUP)
    grp_l = math.gcd(n_lat, DN_GROUP)
    for grp in range(n_ctx // grp_c):
        prep_group(sqc_ref, skc_ref, svc_ref, gc_ref, grp * grp_c, grp_c, n_ctx, 0)

    def prep_body(grp, carry):
        prep_group(sql_ref, skl_ref, svl_ref, gl_ref, grp * grp_l, grp_l, n_lat, n_ctx)
        return carry

    lax.fori_loop(0, n_lat // grp_l, prep_body, 0)

    def scan_body(dst_refs, n_seq, base):
        def body(i, states):
            pos = base + i
            rows = (pl.multiple_of(i * c, c), pl.multiple_of((n_seq - 1 - i) * c, c))
            new = []
            for d in range(2):
                s = states[d]
                sb = _bf(s)
                dst_refs[d][pl.ds(rows[d], c), :] = _dot(qt_ref[d, pos], sb) + o_ref[d, pos]
                new.append(eg_ref[d, pos][0:1, :] * s - _dot(p_ref[d, pos], sb) + n_ref[d, pos])
            return tuple(new)
        return body

    zero = jnp.zeros((LANES, LANES), F32)
    states = lax.fori_loop(0, n_ctx, scan_body((yc_ref, obc_ref), n_ctx, 0), (zero, zero))
    lax.fori_loop(0, n_lat, scan_body((yl_ref, obl_ref), n_lat, n_ctx), states)

    ng = ng_ref[...]

    def finish(y_ref, ob_ref, z_ref):
        n_rows = y_ref.shape[0]
        tile = 256
        for r in range(0, n_rows, tile):
            sl = slice(r, r + tile)
            y_ref[sl, :] = _rms(y_ref[sl, :] + ob_ref[sl, :], ng) * _silu(z_ref[sl, :])

    finish(yc_ref, obc_ref, zc_ref)
    finish(yl_ref, obl_ref, zl_ref)


def _deltanet(pl_f32, pc_f32, conv_w, alog_rows, dtb_rows, ng):
    b, t, _ = pl_f32.shape
    tc = pc_f32.shape[1]
    h = DN_HEADS
    n_tot = (t + tc) // DN_CHUNK

    def specs(n):
        return [pl.BlockSpec((None, n, LANES), lambda bi, hi: (bi, 0, hi)),
                pl.BlockSpec((None, n, LANES), lambda bi, hi: (bi, 0, h + hi)),
                pl.BlockSpec((None, n, LANES), lambda bi, hi: (bi, 0, 2 * h + hi)),
                pl.BlockSpec((None, n, LANES), lambda bi, hi: (bi, 0, 3 * h + hi)),
                pl.BlockSpec((None, n, LANES), lambda bi, hi: (bi, 0, 4 * h + hi))]
    in_specs = specs(t) + specs(tc) + [
        pl.BlockSpec((3, LANES), lambda bi, hi: (0, hi)),
        pl.BlockSpec((3, LANES), lambda bi, hi: (0, h + hi)),
        pl.BlockSpec((3, LANES), lambda bi, hi: (0, 2 * h + hi)),
        pl.BlockSpec((None, 1, LANES), lambda bi, hi: (hi, 0, 0)),
        pl.BlockSpec((None, 1, LANES), lambda bi, hi: (hi, 0, 0)),
        pl.BlockSpec((1, LANES), lambda bi, hi: (0, 0))]
    yl, yc = pl.pallas_call(
        functools.partial(_dn_kernel, n_lat=t // DN_CHUNK, n_ctx=tc // DN_CHUNK),
        grid=(b, h),
        in_specs=in_specs,
        out_specs=[pl.BlockSpec((None, t, LANES), lambda bi, hi: (bi, 0, hi)),
                   pl.BlockSpec((None, tc, LANES), lambda bi, hi: (bi, 0, hi))],
        out_shape=[jax.ShapeDtypeStruct((b, t, h * LANES), F32),
                   jax.ShapeDtypeStruct((b, tc, h * LANES), F32)],
        scratch_shapes=[pltpu.VMEM((t, LANES), F32)] * 3 + [pltpu.VMEM((tc, LANES), F32)] * 3
                       + [pltpu.VMEM((t, LANES), F32), pltpu.VMEM((tc, LANES), F32),
                          pltpu.VMEM((2, n_tot, LANES, LANES), BF16), pltpu.VMEM((2, n_tot, LANES, LANES), F32),
                          pltpu.VMEM((2, n_tot, DN_CHUNK, LANES), BF16), pltpu.VMEM((2, n_tot, DN_CHUNK, LANES), F32),
                          pltpu.VMEM((2, n_tot, 8, LANES), F32)],
        compiler_params=_params("arbitrary", "arbitrary"),
        name="deltanet",
    )(*([pl_f32] * 5 + [pc_f32] * 5 + [conv_w, conv_w, conv_w, alog_rows, dtb_rows, ng.reshape(1, LANES)]))
    return yl, yc


_FF_CHUNKS = ((0, 1536), (1536, D_FF))


def _post_kernel(*refs, final):
    if final:
        x_ref, ya_ref, yb_ref, mod_ref, gf_ref, wa_ref, wb_ref, wg_ref, wu_ref, wd_ref, gfin_ref, o_ref = refs
    else:
        x_ref, ya_ref, yb_ref, mod_ref, gf_ref, wa_ref, wb_ref, wg_ref, wu_ref, wd_ref, o_ref = refs
    mod = mod_ref[...]
    ga1 = mod[:, 2 * D_MODEL:3 * D_MODEL]
    ga2 = mod[:, 5 * D_MODEL:6 * D_MODEL]
    y = _dot(_bf(ya_ref[...]), wa_ref[...]) + _dot(_bf(yb_ref[...]), wb_ref[...])
    x1 = x_ref[...] + ga1 * y
    hb = _bf(_norm_mod(x1, gf_ref[...], mod, 3))
    acc = None
    for lo, hi in _FF_CHUNKS:
        gate = _dot(hb, wg_ref[:, lo:hi])
        up = _dot(hb, wu_ref[:, lo:hi])
        part = _dot(_bf(_silu(gate) * up), wd_ref[lo:hi, :])
        acc = part if acc is None else acc + part
    out = x1 + ga2 * acc
    if final:
        out = _rms(out, gfin_ref[...])
    o_ref[...] = out


def _post(x, ya, yb, mods, gf, wa, wb, wg, wu, wd, gfin, *, mod_row, tm):
    b, t, _ = x.shape
    rows = b * t
    nb = rows // tm
    final = gfin is not None
    na, nbw = ya.shape[-1], yb.shape[-1]

    def const(shape):
        return pl.BlockSpec(shape, lambda i: (0, 0), pipeline_mode=pl.Buffered(1))
    in_specs = [pl.BlockSpec((tm, D_MODEL), lambda i: (i, 0)),
                pl.BlockSpec((tm, na), lambda i: (i, 0)),
                pl.BlockSpec((tm, nbw), lambda i: (i, 0)),
                pl.BlockSpec((None, 1, 6 * D_MODEL), lambda i: (mod_row(i), 0, 0)),
                pl.BlockSpec((1, D_MODEL), lambda i: (0, 0)),
                const((na, D_MODEL)), const((nbw, D_MODEL)),
                const((D_MODEL, D_FF)), const((D_MODEL, D_FF)), const((D_FF, D_MODEL))]
    args = [x.reshape(rows, D_MODEL), ya.reshape(rows, na), yb.reshape(rows, nbw), mods,
            gf.reshape(1, D_MODEL), wa, wb, wg, wu, wd]
    if final:
        in_specs.append(pl.BlockSpec((1, D_MODEL), lambda i: (0, 0)))
        args.append(gfin.reshape(1, D_MODEL))
    out = pl.pallas_call(
        functools.partial(_post_kernel, final=final),
        grid=(nb,),
        in_specs=in_specs,
        out_specs=pl.BlockSpec((tm, D_MODEL), lambda i: (i, 0)),
        out_shape=jax.ShapeDtypeStruct((rows, D_MODEL), F32),
        compiler_params=_params("arbitrary"),
        name="post_ffn",
    )(*args)
    return out.reshape(b, t, D_MODEL)


def _rope_tables(n_tok):
    rows = n_tok // GRID_W
    quarter = HD // 4
    freqs = ROPE_THETA ** (-jnp.arange(quarter, dtype=F32) / quarter)
    row = jnp.repeat(jnp.arange(rows, dtype=F32), GRID_W)
    col = jnp.tile(jnp.arange(GRID_W, dtype=F32), rows)
    ang = jnp.concatenate([row[:, None] * freqs, col[:, None] * freqs], axis=-1)
    cos, sin = jnp.cos(ang), jnp.sin(ang)
    lane = np.arange(LANES)
    d = lane % HD
    src = (d // 32) * quarter + (d % quarter)
    first = (d % 32) < quarter
    c = cos[:, src]
    s = sin[:, src]
    s1 = jnp.where(first[None, :], -s, 0.0)
    s2 = jnp.where(first[None, :], 0.0, s)
    return c, s1, s2


def _place(cols_src, width, dst_lanes):
    idx = np.zeros((width,), np.int32)
    msk = np.zeros((width,), bool)
    for src, dst in zip(cols_src, dst_lanes):
        idx[dst] = src
        msk[dst] = True
    return idx, msk


def _gather_cols(w, idx, msk):
    parts = []
    n = len(idx)
    i = 0
    while i < n:
        j = i
        if not msk[i]:
            while j < n and not msk[j]:
                j += 1
            parts.append(jnp.zeros((w.shape[0], j - i), w.dtype))
        else:
            while j + 1 < n and msk[j + 1] and idx[j + 1] == idx[j] + 1:
                j += 1
            j += 1
            parts.append(w[:, int(idx[i]):int(idx[i]) + (j - i)])
        i = j
    return jnp.concatenate(parts, axis=1)


def _even_layout():
    src, dst = [], []
    for i in range(3072):
        src.append(i)
        dst.append(i)
    gbase = 3072
    for p in range(2):
        for kind in range(4):
            for hh in range(2):
                src.append(gbase + kind * ML_HEADS + 2 * p + hh)
                dst.append(3072 + p * LANES + kind * 2 + hh)
    return _place(src, EVEN_BF + EVEN_F32, dst)


def _odd_layout():
    src, dst = [], []
    per_kv = GQ_HEADS // GQ_KV
    for h in range(GQ_HEADS):
        j = h // per_kv
        for e in range(HD):
            src.append(h * HD + e)
            dst.append(h * LANES + j * HD + e)
    for j in range(GQ_KV):
        for e in range(HD):
            src.append(512 + j * HD + e)
            dst.append(GQ_HEADS * LANES + j * LANES + j * HD + e)
    for e in range(GQ_KV * HD):
        src.append(640 + e)
        dst.append((GQ_HEADS + GQ_KV) * LANES + e)
    for e in range(1536 + 512):
        src.append(768 + e)
        dst.append(ODD_BF + e)
    for h in range(DN_HEADS):
        for d in range(2):
            for kind in range(2):
                src.append(2816 + d * 2 * DN_HEADS + kind * DN_HEADS + h)
                dst.append(ODD_BF + 2048 + h * LANES + d * 2 + kind)
    return _place(src, ODD_BF + ODD_F32, dst)


def kernel(x, c, ctx, c_ctx, w_ada, b_ada, g_mix, g_ffn, w_in_e, b_gate_e, da_lam, da_norm_g, ml_norm_g,
           w_out_e, w_in_o, qk_norm_g, dn_conv, dn_a_log, dn_dt_bias, dn_norm_g, w_out_o, w_gate, w_up,
           w_down, g_final):
    b, t, _ = x.shape
    tc = ctx.shape[1]
    tables = _rope_tables(t)

    cv = jnp.zeros((16, D_MODEL), F32).at[:b].set(c).at[b].set(c_ctx)
    mods = _ada_mods(cv, w_ada, b_ada).reshape(DEPTH, 16, 1, 6 * D_MODEL)

    e_idx, e_msk = _even_layout()
    o_idx, o_msk = _odd_layout()
    tm_l = 512
    tm_c = 256
    bpb = t // tm_l
    lat_row = lambda i: i // bpb
    ctx_row = lambda i: b

    xl, xc = x, ctx
    for layer in range(DEPTH):
        emit_ctx = layer < DEPTH - 1
        m = mods[layer]
        if layer % 2 == 0:
            e = layer // 2
            lam_init = 0.8 - 0.6 * math.exp(-0.3 * layer)
            w = _bf(_gather_cols(w_in_e[e], e_idx, e_msk))
            pl_bf, pl_f = _inproj(xl, m, g_mix[layer], w, [], tables, even=True, seq_len=t, mod_row=lat_row, tm=tm_l)
            pc_bf, pc_f = _inproj(xc, m, g_mix[layer], w, [], None, even=True, seq_len=tc, mod_row=ctx_row, tm=tm_c)
            ya_l = _diff_attn(pl_bf, [pc_bf, pl_bf], da_lam[e], da_norm_g[e], lam_init, 256, True)
            gbias = jnp.zeros((2, 1, LANES), F32)
            gb = b_gate_e[e].reshape(4, 2, 2)
            gbias = gbias.at[:, 0, :8].set(gb.transpose(1, 0, 2).reshape(2, 8))
            yb_l, yb_c = _mlstm(pl_f, pc_f, gbias, ml_norm_g[e])
            if emit_ctx:
                ya_c = _diff_attn(pc_bf, [pc_bf], da_lam[e], da_norm_g[e], lam_init, tc, False)
            w_out = _bf(w_out_e[e])
            wa, wb = w_out[:DA_HEADS * LANES], w_out[DA_HEADS * LANES:]
        else:
            o = layer // 2
            w = _bf(_gather_cols(w_in_o[o], o_idx, o_msk))
            qkg = jnp.zeros((2 * GQ_KV, LANES), F32)
            for j in range(GQ_KV):
                qkg = qkg.at[j, j * HD:(j + 1) * HD].set(qk_norm_g[o, 0])
                qkg = qkg.at[GQ_KV + j, j * HD:(j + 1) * HD].set(qk_norm_g[o, 1])
            pl_bf, pl_f = _inproj(xl, m, g_mix[layer], w, [qkg], tables, even=False, seq_len=t, mod_row=lat_row, tm=tm_l)
            pc_bf, pc_f = _inproj(xc, m, g_mix[layer], w, [qkg], None, even=False, seq_len=tc, mod_row=ctx_row, tm=tm_c)
            ya_l = _gqa_attn(pl_bf, [pc_bf, pl_bf], 256, True)
            alog = jnp.zeros((DN_HEADS, 1, LANES), F32)
            dtb = jnp.zeros((DN_HEADS, 1, LANES), F32)
            for d in range(2):
                alog = alog.at[:, 0, 2 * d + 1].set(dn_a_log[o, d])
                dtb = dtb.at[:, 0, 2 * d + 1].set(dn_dt_bias[o, d])
            yb_l, yb_c = _deltanet(pl_f, pc_f, dn_conv[o], alog, dtb, dn_norm_g[o])
            if emit_ctx:
                ya_c = _gqa_attn(pc_bf, [pc_bf], tc, False)
            w_out = _bf(w_out_o[o])
            wa, wb = w_out[:GQ_HEADS * HD], w_out[GQ_HEADS * HD:]
        wg, wu, wd = _bf(w_gate[layer]), _bf(w_up[layer]), _bf(w_down[layer])
        gfin = g_final if layer == DEPTH - 1 else None
        xl = _post(xl, ya_l, yb_l, m, g_ffn[layer], wa, wb, wg, wu, wd, gfin, mod_row=lat_row, tm=tm_l)
        if emit_ctx:
            xc = _post(xc, ya_c, yb_c, m, g_ffn[layer], wa, wb, wg, wu, wd, None, mod_row=ctx_row, tm=tm_c)
    return xl
```

```python
import functools
import math

import jax
import jax.numpy as jnp
import numpy as np
from jax import lax
from jax.experimental import pallas as pl
from jax.experimental.pallas import tpu as pltpu

F32 = jnp.float32
BF16 = jnp.bfloat16

D_MODEL = 1024
DEPTH = 4
GRID_W = 64
ROPE_THETA = 10000.0
EPS = 1e-6
HD = 64
Q_BLOCK = 128
LANES = 128
DA_HEADS = 4
ML_HEADS = 4
GQ_HEADS = 8
GQ_KV = 2
DN_HEADS = 4
D_FF = 2816
ML_CHUNK = 128
DN_CHUNK = 64
DN_GROUP = 8
VMEM_LIMIT = 56 * 1024 * 1024

EVEN_BF = 1536
EVEN_F32 = 1792
ODD_BF = 1408
ODD_F32 = 2560


def _bf(x):
    return x.astype(BF16)


def _dot(a, b):
    return jnp.dot(a, b, preferred_element_type=F32)


def _dot_nt(a, b):
    return lax.dot_general(a, b, (((1,), (1,)), ((), ())), preferred_element_type=F32)


def _dot_tn(a, b):
    return lax.dot_general(a, b, (((0,), (0,)), ((), ())), preferred_element_type=F32)


def _split2(x):
    hi = _bf(x)
    return hi, _bf(x - hi.astype(F32))


def _split3(x):
    hi = _bf(x)
    r = x - hi.astype(F32)
    mid = _bf(r)
    return hi, mid, _bf(r - mid.astype(F32))


def _dot_x3(a, b):
    ah, al = _split2(a)
    bh, bl = _split2(b)
    return _dot(ah, bh) + (_dot(ah, bl) + _dot(al, bh))


def _sel_dot(sel, x):
    h, m, l = _split3(x)
    return _dot(sel, h) + (_dot(sel, m) + _dot(sel, l))


def _sel_dot_nt(sel, x):
    h, m, l = _split3(x)
    return _dot_nt(sel, h) + (_dot_nt(sel, m) + _dot_nt(sel, l))


def _sigmoid(x):
    return 1.0 / (1.0 + jnp.exp(-x))


def _silu(x):
    return x * _sigmoid(x)


def _log_sigmoid(x):
    return jnp.minimum(x, 0.0) - jnp.log(1.0 + jnp.exp(-jnp.abs(x)))


def _softplus(x):
    return jnp.maximum(x, 0.0) + jnp.log(1.0 + jnp.exp(-jnp.abs(x)))


def _rms(x, g, n=None):
    n = x.shape[-1] if n is None else n
    ss = jnp.sum(x * x, axis=-1, keepdims=True)
    return x * lax.rsqrt(ss * (1.0 / n) + EPS) * g


def _norm_mod(x, g, mod, slot):
    sh = mod[:, slot * D_MODEL:(slot + 1) * D_MODEL]
    sc = mod[:, (slot + 1) * D_MODEL:(slot + 2) * D_MODEL]
    return _rms(x, g) * (1.0 + sc) + sh


def _rope(x, c, s1, s2):
    return x * c + pltpu.roll(x, LANES - 16, 1) * s1 + pltpu.roll(x, 16, 1) * s2


def _params(*sem):
    return pltpu.CompilerParams(dimension_semantics=sem, vmem_limit_bytes=VMEM_LIMIT)


def _ada_kernel(cv_ref, w_ref, b_ref, o_ref):
    cv = cv_ref[...]
    o_ref[...] = _dot_x3(_silu(cv), w_ref[...]) + b_ref[...]


def _ada_mods(cv, w_ada, b_ada):
    rows = cv.shape[0]
    nb = 6
    return pl.pallas_call(
        _ada_kernel,
        grid=(DEPTH, nb),
        in_specs=[pl.BlockSpec((rows, D_MODEL), lambda l, j: (0, 0)),
                  pl.BlockSpec((None, D_MODEL, D_MODEL), lambda l, j: (l, 0, j)),
                  pl.BlockSpec((None, 1, D_MODEL), lambda l, j: (l, 0, j))],
        out_specs=pl.BlockSpec((None, rows, D_MODEL), lambda l, j: (l, 0, j)),
        out_shape=jax.ShapeDtypeStruct((DEPTH, rows, 6 * D_MODEL), F32),
        compiler_params=_params("arbitrary", "arbitrary"),
        name="ada_mods",
    )(cv, w_ada, b_ada.reshape(DEPTH, 1, 6 * D_MODEL))


def _inproj_even_kernel(*refs, rope):
    if rope:
        x_ref, mod_ref, g_ref, w_ref, rc_ref, s1_ref, s2_ref, o1_ref, o2_ref = refs
    else:
        x_ref, mod_ref, g_ref, w_ref, o1_ref, o2_ref = refs
    hb = _bf(_norm_mod(x_ref[...], g_ref[...], mod_ref[...], 0))
    for seg in range(4):
        p = _dot(hb, w_ref[:, seg * 256:(seg + 1) * 256])
        for half in range(2):
            gi = seg * 2 + half
            xg = p[:, half * LANES:(half + 1) * LANES]
            if rope:
                xg = _rope(xg, rc_ref[...], s1_ref[...], s2_ref[...])
            if gi < DA_HEADS:
                xg = xg * (HD ** -0.5)
            o1_ref[:, gi * LANES:(gi + 1) * LANES] = _bf(xg)
    o1_ref[:, 1024:EVEN_BF] = _bf(_dot(hb, w_ref[:, 1024:EVEN_BF]))
    o2_ref[...] = _dot(hb, w_ref[:, EVEN_BF:])


def _inproj_odd_kernel(*refs, rope):
    if rope:
        x_ref, mod_ref, g_ref, w_ref, qkg_ref, rc_ref, s1_ref, s2_ref, o1_ref, o2_ref = refs
    else:
        x_ref, mod_ref, g_ref, w_ref, qkg_ref, o1_ref, o2_ref = refs
    hb = _bf(_norm_mod(x_ref[...], g_ref[...], mod_ref[...], 0))
    for seg in range(5):
        p = _dot(hb, w_ref[:, seg * 256:(seg + 1) * 256])
        for half in range(2):
            gi = seg * 2 + half
            xg = p[:, half * LANES:(half + 1) * LANES]
            if gi < GQ_HEADS:
                grow = gi // (GQ_HEADS // GQ_KV)
            else:
                grow = GQ_KV + (gi - GQ_HEADS)
            xg = _rms(xg, qkg_ref[grow:grow + 1, :], HD)
            if rope:
                xg = _rope(xg, rc_ref[...], s1_ref[...], s2_ref[...])
            if gi < GQ_HEADS:
                xg = xg * (HD ** -0.5)
            o1_ref[:, gi * LANES:(gi + 1) * LANES] = _bf(xg)
    o1_ref[:, 1280:ODD_BF] = _bf(_dot(hb, w_ref[:, 1280:ODD_BF]))
    o2_ref[...] = _dot(hb, w_ref[:, ODD_BF:])


def _inproj(x, mods, g, w, extra, tables, *, even, seq_len, mod_row, tm):
    b, t, _ = x.shape
    rows = b * t
    nb = rows // tm
    bpb = max(t // tm, 1)
    n_bf, n_f32 = (EVEN_BF, EVEN_F32) if even else (ODD_BF, ODD_F32)
    rope = tables is not None
    kern = functools.partial(_inproj_even_kernel if even else _inproj_odd_kernel, rope=rope)
    in_specs = [pl.BlockSpec((tm, D_MODEL), lambda i: (i, 0)),
                pl.BlockSpec((None, 1, 6 * D_MODEL), lambda i: (mod_row(i), 0, 0)),
                pl.BlockSpec((1, D_MODEL), lambda i: (0, 0)),
                pl.BlockSpec((D_MODEL, n_bf + n_f32), lambda i: (0, 0))]
    args = [x.reshape(rows, D_MODEL), mods, g.reshape(1, D_MODEL), w]
    for e in extra:
        in_specs.append(pl.BlockSpec(e.shape, lambda i: (0, 0)))
        args.append(e)
    if rope:
        for tab in tables:
            in_specs.append(pl.BlockSpec((tm, LANES), lambda i: (i % bpb, 0)))
            args.append(tab)
    o1, o2 = pl.pallas_call(
        kern,
        grid=(nb,),
        in_specs=in_specs,
        out_specs=[pl.BlockSpec((tm, n_bf), lambda i: (i, 0)),
                   pl.BlockSpec((tm, n_f32), lambda i: (i, 0))],
        out_shape=[jax.ShapeDtypeStruct((rows, n_bf), BF16),
                   jax.ShapeDtypeStruct((rows, n_f32), F32)],
        compiler_params=_params("arbitrary"),
        name="inproj_even" if even else "inproj_odd",
    )(*args)
    return o1.reshape(b, t, n_bf), o2.reshape(b, t, n_f32)


def _with_ones(v):
    return jnp.concatenate([v, jnp.ones_like(v)], axis=1)


def _softmax_pv(q, kvs):
    ss = [_dot_nt(q, k) for k, _ in kvs]
    mx = ss[0].max(axis=-1, keepdims=True)
    for s in ss[1:]:
        mx = jnp.maximum(mx, s.max(axis=-1, keepdims=True))
    acc = None
    for s, (_, v) in zip(ss, kvs):
        o = _dot(_bf(jnp.exp(s - mx)), v)
        acc = o if acc is None else acc + o
    return acc[:, :LANES] / acc[:, LANES:]


def _store_heads(o_ref, o, group, blocked):
    if not blocked:
        o_ref[group] = o
        return
    n_blocks = o_ref.shape[1] // Q_BLOCK
    per_step = o.shape[0] // Q_BLOCK
    for jj in range(per_step):
        j = pl.program_id(1) * per_step + jj
        o_ref[group, pl.ds(j, Q_BLOCK, stride=n_blocks), :] = o[jj * Q_BLOCK:(jj + 1) * Q_BLOCK, :]


def _attn_out(b, t, groups, tq, blocked):
    shape = jax.ShapeDtypeStruct((b, groups, t, LANES), F32)
    if blocked:
        return pl.BlockSpec((None, groups, t, LANES), lambda bi, qi: (bi, 0, 0, 0)), shape
    return pl.BlockSpec((None, groups, tq, LANES), lambda bi, qi: (bi, 0, qi, 0)), shape


def _diff_attn_kernel(*refs, n_kv, lam_init, blocked):
    q_ref = refs[0]
    kv_refs = refs[1:1 + 2 * n_kv]
    lam_ref, ng_ref, o_ref = refs[1 + 2 * n_kv:]
    lane = lax.broadcasted_iota(jnp.int32, (1, LANES), 1)
    lf = lam_ref[...]
    s1 = jnp.sum(lf[0:1, :] * lf[1:2, :], axis=-1, keepdims=True)
    s2 = jnp.sum(lf[2:3, :] * lf[3:4, :], axis=-1, keepdims=True)
    lmb = jnp.exp(s1) - jnp.exp(s2) + lam_init
    ng = ng_ref[...]
    for h in range(DA_HEADS):
        hs = slice(h * LANES, (h + 1) * LANES)
        q = q_ref[:, hs]
        kvs = [(kv_refs[2 * i][:, hs], _with_ones(kv_refs[2 * i + 1][:, hs])) for i in range(n_kv)]
        outs = []
        for m in range(2):
            in_map = (lane >= m * HD) & (lane < (m + 1) * HD)
            qm = jnp.where(in_map, q, jnp.zeros_like(q))
            outs.append(_softmax_pv(qm, kvs))
        o = outs[0] - lmb * outs[1]
        _store_heads(o_ref, _rms(o, ng) * (1.0 - lam_init), h, blocked)


def _diff_attn(q_src, kv_srcs, lam, ng, lam_init, tq, blocked):
    b, t, _ = q_src.shape
    nq = t // tq
    width = DA_HEADS * LANES
    in_specs = [pl.BlockSpec((None, tq, width), lambda bi, qi: (bi, qi, 0))]
    args = [q_src]
    for src in kv_srcs:
        n = src.shape[1]
        in_specs.append(pl.BlockSpec((None, n, width), lambda bi, qi: (bi, 0, 1)))
        in_specs.append(pl.BlockSpec((None, n, width), lambda bi, qi: (bi, 0, 2)))
        args += [src, src]
    in_specs += [pl.BlockSpec((4, HD), lambda bi, qi: (0, 0)),
                 pl.BlockSpec((1, LANES), lambda bi, qi: (0, 0))]
    args += [lam, ng.reshape(1, LANES)]
    out_spec, out_shape = _attn_out(b, t, DA_HEADS, tq, blocked)
    return pl.pallas_call(
        functools.partial(_diff_attn_kernel, n_kv=len(kv_srcs), lam_init=lam_init, blocked=blocked),
        grid=(b, nq),
        in_specs=in_specs,
        out_specs=out_spec,
        out_shape=out_shape,
        compiler_params=_params("arbitrary", "arbitrary"),
        name="diff_attn",
    )(*args)


def _gqa_kernel(*refs, n_kv, blocked):
    q_ref = refs[0]
    kv_refs = refs[1:1 + 2 * n_kv]
    o_ref = refs[1 + 2 * n_kv]
    lane = lax.broadcasted_iota(jnp.int32, (1, LANES), 1)
    low = lane < HD
    per_kv = GQ_HEADS // GQ_KV
    vals = [_with_ones(kv_refs[2 * i + 1][...]) for i in range(n_kv)]
    for pair in range(GQ_HEADS // 2):
        j = (2 * pair) // per_kv
        kvs = [(kv_refs[2 * i][:, j * LANES:(j + 1) * LANES], vals[i]) for i in range(n_kv)]
        o_a = _softmax_pv(q_ref[:, (2 * pair) * LANES:(2 * pair + 1) * LANES], kvs)
        o_b = _softmax_pv(q_ref[:, (2 * pair + 1) * LANES:(2 * pair + 2) * LANES], kvs)
        if j == 0:
            packed = jnp.where(low, o_a, pltpu.roll(o_b, HD, 1))
        else:
            packed = jnp.where(low, pltpu.roll(o_a, HD, 1), o_b)
        _store_heads(o_ref, packed, pair, blocked)


def _gqa_attn(q_src, kv_srcs, tq, blocked):
    b, t, _ = q_src.shape
    nq = t // tq
    qw = GQ_HEADS * LANES
    in_specs = [pl.BlockSpec((None, tq, qw), lambda bi, qi: (bi, qi, 0))]
    args = [q_src]
    for src in kv_srcs:
        n = src.shape[1]
        in_specs.append(pl.BlockSpec((None, n, GQ_KV * LANES), lambda bi, qi: (bi, 0, qw // (GQ_KV * LANES))))
        in_specs.append(pl.BlockSpec((None, n, LANES), lambda bi, qi: (bi, 0, (qw + GQ_KV * LANES) // LANES)))
        args += [src, src]
    out_spec, out_shape = _attn_out(b, t, GQ_HEADS // 2, tq, blocked)
    return pl.pallas_call(
        functools.partial(_gqa_kernel, n_kv=len(kv_srcs), blocked=blocked),
        grid=(b, nq),
        in_specs=in_specs,
        out_specs=out_spec,
        out_shape=out_shape,
        compiler_params=_params("arbitrary", "arbitrary"),
        name="gqa_attn",
    )(*args)


def _mlstm_kernel(ql_ref, kl_ref, vl_ref, ol_ref, gl_ref, qc_ref, kc_ref, vc_ref, oc_ref, gc_ref,
                  gb_ref, ng_ref, yl_ref, yc_ref, hbl_ref, hbc_ref, cst_ref, mst_ref, *, n_lat, n_ctx):
    c = ML_CHUNK
    lane = lax.broadcasted_iota(jnp.int32, (1, LANES), 1)
    ri = lax.broadcasted_iota(jnp.int32, (c, c), 0)
    ci = lax.broadcasted_iota(jnp.int32, (c, c), 1)
    masks = (ci <= ri, ci >= ri)
    cum_mat = _bf(masks[0].astype(F32))
    row_id = lax.broadcasted_iota(jnp.int32, (c, LANES), 0)
    ones_blk = jnp.ones((c, LANES), F32)
    gb = gb_ref[...]
    k3 = lax.broadcasted_iota(jnp.int32, (2 * LANES, LANES), 0) & (LANES - 1)
    c3 = lax.broadcasted_iota(jnp.int32, (c, 2 * LANES), 1) & (LANES - 1)
    units = [(d, hh) for d in range(2) for hh in range(2)]
    gate_lane = [d * 4 + hh for d, hh in units]
    sel_t = [_bf((k3 == cl).astype(F32)) for cl in gate_lane]
    sel_r = [_bf((c3 == cl).astype(F32)) for cl in gate_lane]
    head_mask = [(lane >= hh * HD) & (lane < (hh + 1) * HD) for hh in range(2)]

    cst_ref[...] = jnp.zeros_like(cst_ref)
    mst_ref[...] = jnp.zeros_like(mst_ref)

    def cummax_rows(x, reverse):
        k = 1
        while k < c:
            if reverse:
                sh = jnp.where(row_id < c - k, pltpu.roll(x, c - k, 0), -jnp.inf)
            else:
                sh = jnp.where(row_id >= k, pltpu.roll(x, k, 0), -jnp.inf)
            x = jnp.maximum(x, sh)
            k *= 2
        return x

    def gate_prep(g_ref, r0, d):
        g = g_ref[pl.ds(r0, c), :] + gb
        lf = _log_sigmoid(g)
        cum = _sel_dot(cum_mat, lf)
        if d == 1:
            cum = jnp.sum(lf, axis=0, keepdims=True) - cum + lf
        bc = pltpu.roll(cum, LANES - 2, 1)
        w = g - bc
        cmx = cummax_rows(w, d == 1)
        edge = slice(c - 1, c) if d == 0 else slice(0, 1)
        h, l = _split2(jnp.concatenate([w, cmx, bc], axis=0))
        return jnp.concatenate([h, l], axis=1), bc[edge, :], cmx[edge, :]

    def chunk_step(q_ref, k_ref, v_ref, g_ref, dst_refs, rows):
        preps = [gate_prep(g_ref, rows[d], d) for d in range(2)]
        q2 = [q_ref[pl.ds(rows[d], c), :] for d in range(2)]
        k2 = [k_ref[pl.ds(rows[d], c), :] * (HD ** -0.5) for d in range(2)]
        v2 = [v_ref[pl.ds(rows[d], c), :] for d in range(2)]
        qm = [_bf(jnp.where(head_mask[hh], q2[d], 0.0)) for d, hh in units]
        km = [_bf(jnp.where(head_mask[hh], k2[d], 0.0)) for d, hh in units]
        vh = [v2[d][:, hh * LANES:(hh + 1) * LANES] for d, hh in units]
        bb = [_dot(preps[d][0], sel_t[i]) for i, (d, hh) in enumerate(units)]
        w_b = [x[0:c] for x in bb]
        cm_b = [x[c:2 * c] for x in bb]
        bc_b = [x[2 * c:3 * c] for x in bb]
        w_row = [_dot_nt(sel_r[i], preps[d][0][0:c]) for i, (d, hh) in enumerate(units)]
        qk = _each(_dot_nt, qm, km)
        p = [jnp.exp(jnp.where(masks[d], wr - cm, -jnp.inf)) * s
             for (d, hh), wr, cm, s in zip(units, w_row, cm_b, qk)]
        num_loc = [_dot(_bf(pi), _bf(jnp.concatenate([v, ones_blk], axis=1))) for pi, v in zip(p, vh)]

        m_st = [mst_ref[i][0:1, 0:1] for i in range(4)]
        c_st = [cst_ref[i] for i in range(4)]
        b_end = [preps[d][1][:, cl:cl + 1] for (d, hh), cl in zip(units, gate_lane)]
        w_max = [preps[d][2][:, cl:cl + 1] for (d, hh), cl in zip(units, gate_lane)]
        m_new = [jnp.maximum(be + ms, be + wm) for be, ms, wm in zip(b_end, m_st, w_max)]
        top = _each(jnp.maximum, m_st, cm_b)
        inter = [jnp.exp(ms - t) for ms, t in zip(m_st, top)]
        s_loc = [jnp.exp(cm - t) for cm, t in zip(cm_b, top)]
        floor = [jnp.exp(-(bc + t)) for bc, t in zip(bc_b, top)]
        q_c = [_dot(q, _bf(cs)) for q, cs in zip(qm, c_st)]
        for i, (d, hh) in enumerate(units):
            num = inter[i] * q_c[i][:, :LANES] + s_loc[i] * num_loc[i][:, :LANES]
            den = inter[i] * q_c[i][:, LANES:] + s_loc[i] * num_loc[i][:, LANES:]
            dst_refs[d][pl.ds(rows[d], c), hh * LANES:(hh + 1) * LANES] = num / jnp.maximum(jnp.abs(den), floor[i])
        w_in = [jnp.exp(be + wb - mn) for be, wb, mn in zip(b_end, w_b, m_new)]
        kv = [_dot_tn(k, _bf(jnp.concatenate([w * v, w], axis=1))) for k, w, v in zip(km, w_in, vh)]
        for i in range(4):
            cst_ref[i] = jnp.exp(b_end[i] + m_st[i] - m_new[i]) * c_st[i] + kv[i]
            mst_ref[i] = jnp.broadcast_to(m_new[i], mst_ref.shape[1:])

    for cc in range(n_ctx):
        chunk_step(qc_ref, kc_ref, vc_ref, gc_ref, (yc_ref, hbc_ref), (cc * c, (n_ctx - 1 - cc) * c))

    def body(cc, carry):
        rows = (pl.multiple_of(cc * c, c), pl.multiple_of((n_lat - 1 - cc) * c, c))
        chunk_step(ql_ref, kl_ref, vl_ref, gl_ref, (yl_ref, hbl_ref), rows)
        return carry

    lax.fori_loop(0, n_lat, body, 0, unroll=2)

    ng = ng_ref[...]

    def finish(y_ref, hb_ref, o_ref, n_rows):
        tile = 256
        for r in range(0, n_rows, tile):
            for hh in range(2):
                sl = (slice(r, r + tile), slice(hh * LANES, (hh + 1) * LANES))
                hsum = y_ref[sl] + hb_ref[sl]
                y_ref[sl] = _rms(hsum, ng) * _sigmoid(o_ref[sl])

    finish(yc_ref, hbc_ref, oc_ref, n_ctx * c)
    finish(yl_ref, hbl_ref, ol_ref, n_lat * c)


def _mlstm(pl_f32, pc_f32, gate_bias, ng):
    b, t, _ = pl_f32.shape
    tc = pc_f32.shape[1]
    n_pairs = ML_HEADS // 2
    def specs(n):
        return [pl.BlockSpec((None, n, LANES), lambda bi, p: (bi, 0, p)),
                pl.BlockSpec((None, n, LANES), lambda bi, p: (bi, 0, 2 + p)),
                pl.BlockSpec((None, n, 2 * LANES), lambda bi, p: (bi, 0, 2 + p)),
                pl.BlockSpec((None, n, 2 * LANES), lambda bi, p: (bi, 0, 4 + p)),
                pl.BlockSpec((None, n, LANES), lambda bi, p: (bi, 0, 12 + p))]
    in_specs = specs(t) + specs(tc) + [pl.BlockSpec((None, 1, LANES), lambda bi, p: (p, 0, 0)),
                                      pl.BlockSpec((1, LANES), lambda bi, p: (0, 0))]
    yl, yc = pl.pallas_call(
        functools.partial(_mlstm_kernel, n_lat=t // ML_CHUNK, n_ctx=tc // ML_CHUNK),
        grid=(b, n_pairs),
        in_specs=in_specs,
        out_specs=[pl.BlockSpec((None, t, 2 * LANES), lambda bi, p: (bi, 0, p)),
                   pl.BlockSpec((None, tc, 2 * LANES), lambda bi, p: (bi, 0, p))],
        out_shape=[jax.ShapeDtypeStruct((b, t, ML_HEADS * LANES), F32),
                   jax.ShapeDtypeStruct((b, tc, ML_HEADS * LANES), F32)],
        scratch_shapes=[pltpu.VMEM((t, 2 * LANES), F32), pltpu.VMEM((tc, 2 * LANES), F32),
                        pltpu.VMEM((4, LANES, 2 * LANES), F32), pltpu.VMEM((4, 8, LANES), F32)],
        compiler_params=_params("arbitrary", "arbitrary"),
        name="mlstm",
    )(*([pl_f32] * 5 + [pc_f32] * 5 + [gate_bias, ng.reshape(1, LANES)]))
    return yl, yc


def _each(f, *lists):
    return [f(*xs) for xs in zip(*lists)]


def _unit_tri_inverse(a, eye, blk16, blk32):
    def mm(u, w):
        return _dot(_bf(u), _bf(w))

    d = [jnp.where(blk16, ai, 0.0) for ai in a]
    x = [eye - di for di in d]
    p = _each(mm, d, d)
    for step in range(3):
        x = _each(lambda xi, ti: xi + ti, x, _each(mm, x, p))
        if step < 2:
            p = _each(mm, p, p)
    e = [jnp.where(blk32 & jnp.logical_not(blk16), ai, 0.0) for ai in a]
    x = _each(lambda xi, ti: xi - ti, x, _each(mm, x, _each(mm, e, x)))
    e = [jnp.where(blk32, 0.0, ai) for ai in a]
    x = _each(lambda xi, ti: xi - ti, x, _each(mm, x, _each(mm, e, x)))
    return x


def _dn_kernel(ql_ref, kl_ref, vl_ref, zl_ref, gl_ref, qc_ref, kc_ref, vc_ref, zc_ref, gc_ref,
               wq_ref, wk_ref, wv_ref, alog_ref, dtb_ref, ng_ref, yl_ref, yc_ref,
               sql_ref, skl_ref, svl_ref, sqc_ref, skc_ref, svc_ref, obl_ref, obc_ref,
               p_ref, n_ref, qt_ref, o_ref, eg_ref, *, n_lat, n_ctx):
    c = DN_CHUNK
    lane = lax.broadcasted_iota(jnp.int32, (1, LANES), 1)
    ri = lax.broadcasted_iota(jnp.int32, (c, c), 0)
    ci = lax.broadcasted_iota(jnp.int32, (c, c), 1)
    incl = (ci <= ri, ci >= ri)
    strict = (ci < ri, ci > ri)
    cums = tuple(_bf(m.astype(F32)) for m in incl)
    eye = (ci == ri).astype(F32)
    blk16 = (ri >> 4) == (ci >> 4)
    blk32 = (ri >> 5) == (ci >> 5)
    neg_a = -jnp.exp(alog_ref[...])
    dtb = dtb_ref[...]

    def conv_prep(x_ref, w_ref, dst_ref, kind):
        n = x_ref.shape[0]
        x = x_ref[...]
        rows = lax.broadcasted_iota(jnp.int32, (n, LANES), 0)
        prev = jnp.where(rows == 0, 0.0, pltpu.roll(x, 1, 0))
        nxt = jnp.where(rows == n - 1, 0.0, pltpu.roll(x, n - 1, 0))
        w = w_ref[...]
        y = _silu(prev * w[0:1, :] + x * w[1:2, :] + nxt * w[2:3, :])
        if kind != "v":
            y = y * lax.rsqrt(jnp.sum(y * y, axis=-1, keepdims=True) + EPS)
        if kind == "q":
            y = y * (LANES ** -0.5)
        dst_ref[...] = y

    conv_prep(qc_ref, wq_ref, sqc_ref, "q")
    conv_prep(kc_ref, wk_ref, skc_ref, "k")
    conv_prep(vc_ref, wv_ref, svc_ref, "v")
    conv_prep(ql_ref, wq_ref, sql_ref, "q")
    conv_prep(kl_ref, wk_ref, skl_ref, "k")
    conv_prep(vl_ref, wv_ref, svl_ref, "v")
    sels = tuple(jnp.broadcast_to(_bf((lane == 2 * d + 1).astype(F32)), (c, LANES)) for d in range(2))

    def prep_group(q_ref, k_ref, v_ref, g_ref, c0, n_grp, n_seq, base):
        chunks = range(n_grp)
        units = [(u, d) for u in chunks for d in range(2)]
        rows = [pl.multiple_of((c0 + u) * c, c) for u in chunks]
        q = [q_ref[pl.ds(r, c), :] for r in rows]
        k = [k_ref[pl.ds(r, c), :] for r in rows]
        v = [v_ref[pl.ds(r, c), :] for r in rows]
        g = [g_ref[pl.ds(r, c), :] for r in rows]
        qb = [_bf(x) for x in q]
        kb = [_bf(x) for x in k]
        kk = _each(_dot_nt, kb, kb)
        qk = _each(_dot_nt, qb, kb)
        sig = [_sigmoid(x) for x in g]
        gdec = [neg_a * _softplus(x + dtb) for x in g]
        tot = [jnp.sum(x, axis=0, keepdims=True) for x in gdec]
        cum_f = [_sel_dot(cums[0], x) for x in gdec]
        cum = [jnp.where(lane < 2, cf, t - cf + x) for cf, t, x in zip(cum_f, tot, gdec)]
        gcum = [cum[u][:, 2 * d + 1:2 * d + 2] for u, d in units]
        g_end = [tot[u][:, 2 * d + 1:2 * d + 2] for u, d in units]
        beta = [sig[u][:, 2 * d:2 * d + 1] for u, d in units]
        g_row = [_sel_dot_nt(sels[d], cum[u]) for u, d in units]
        decay = [jnp.exp(jnp.where(incl[d], gc - gr, -jnp.inf)) for (u, d), gc, gr in zip(units, gcum, g_row)]
        a = [jnp.where(strict[d], bt * dc * kk[u], 0.0) for (u, d), bt, dc in zip(units, beta, decay)]
        tinv = _unit_tri_inverse(a, eye, blk16, blk32)
        gam = [jnp.exp(x) for x in gcum]
        rhs = [jnp.concatenate([bt * v[u], (bt * gm) * k[u]], axis=1) for (u, d), bt, gm in zip(units, beta, gam)]
        sol = [_bf(_dot(_bf(ti), _bf(r))) for ti, r in zip(tinv, rhs)]
        attn = [_bf(qk[u] * dc) for (u, d), dc in zip(units, decay)]
        kdec = [_bf(k[u] * jnp.exp(ge - gc)) for (u, d), ge, gc in zip(units, g_end, gcum)]
        qo = _each(_dot, attn, sol)
        pn = _each(_dot_tn, kdec, sol)
        for i, (u, d) in enumerate(units):
            chunk = c0 + u
            pos = base + (chunk if d == 0 else n_seq - 1 - chunk)
            p_ref[d, pos] = _bf(pn[i][:, LANES:])
            n_ref[d, pos] = pn[i][:, :LANES]
            qt_ref[d, pos] = _bf(gam[i] * q[u] - qo[i][:, LANES:])
            o_ref[d, pos] = qo[i][:, :LANES]
            eg_ref[d, pos] = jnp.broadcast_to(jnp.exp(g_end[i]), (8, LANES))

    grp_c = math.gcd(n_ctx, DN_GROUP)
    grp_l = math.gcd(n_lat, DN_GROUP)
    for grp in range(n_ctx // grp_c):
        prep_group(sqc_ref, skc_ref, svc_ref, gc_ref, grp * grp_c, grp_c, n_ctx, 0)

    def prep_body(grp, carry):
        prep_group(sql_ref, skl_ref, svl_ref, gl_ref, grp * grp_l, grp_l, n_lat, n_ctx)
        return carry

    lax.fori_loop(0, n_lat // grp_l, prep_body, 0)

    def scan_body(dst_refs, n_seq, base):
        def body(i, states):
            pos = base + i
            rows = (pl.multiple_of(i * c, c), pl.multiple_of((n_seq - 1 - i) * c, c))
            new = []
            for d in range(2):
                s = states[d]
                sb = _bf(s)
                dst_refs[d][pl.ds(rows[d], c), :] = _dot(qt_ref[d, pos], sb) + o_ref[d, pos]
                new.append(eg_ref[d, pos][0:1, :] * s - _dot(p_ref[d, pos], sb) + n_ref[d, pos])
            return tuple(new)
        return body

    zero = jnp.zeros((LANES, LANES), F32)
    states = lax.fori_loop(0, n_ctx, scan_body((yc_ref, obc_ref), n_ctx, 0), (zero, zero))
    lax.fori_loop(0, n_lat, scan_body((yl_ref, obl_ref), n_lat, n_ctx), states, unroll=4)

    ng = ng_ref[...]

    def finish(y_ref, ob_ref, z_ref):
        n_rows = y_ref.shape[0]
        tile = 256
        for r in range(0, n_rows, tile):
            sl = slice(r, r + tile)
            y_ref[sl, :] = _rms(y_ref[sl, :] + ob_ref[sl, :], ng) * _silu(z_ref[sl, :])

    finish(yc_ref, obc_ref, zc_ref)
    finish(yl_ref, obl_ref, zl_ref)


def _deltanet(pl_f32, pc_f32, conv_w, alog_rows, dtb_rows, ng):
    b, t, _ = pl_f32.shape
    tc = pc_f32.shape[1]
    h = DN_HEADS
    n_tot = (t + tc) // DN_CHUNK

    def specs(n):
        return [pl.BlockSpec((None, n, LANES), lambda bi, hi: (bi, 0, hi)),
                pl.BlockSpec((None, n, LANES), lambda bi, hi: (bi, 0, h + hi)),
                pl.BlockSpec((None, n, LANES), lambda bi, hi: (bi, 0, 2 * h + hi)),
                pl.BlockSpec((None, n, LANES), lambda bi, hi: (bi, 0, 3 * h + hi)),
                pl.BlockSpec((None, n, LANES), lambda bi, hi: (bi, 0, 4 * h + hi))]
    in_specs = specs(t) + specs(tc) + [
        pl.BlockSpec((3, LANES), lambda bi, hi: (0, hi)),
        pl.BlockSpec((3, LANES), lambda bi, hi: (0, h + hi)),
        pl.BlockSpec((3, LANES), lambda bi, hi: (0, 2 * h + hi)),
        pl.BlockSpec((None, 1, LANES), lambda bi, hi: (hi, 0, 0)),
        pl.BlockSpec((None, 1, LANES), lambda bi, hi: (hi, 0, 0)),
        pl.BlockSpec((1, LANES), lambda bi, hi: (0, 0))]
    yl, yc = pl.pallas_call(
        functools.partial(_dn_kernel, n_lat=t // DN_CHUNK, n_ctx=tc // DN_CHUNK),
        grid=(b, h),
        in_specs=in_specs,
        out_specs=[pl.BlockSpec((None, t, LANES), lambda bi, hi: (bi, 0, hi)),
                   pl.BlockSpec((None, tc, LANES), lambda bi, hi: (bi, 0, hi))],
        out_shape=[jax.ShapeDtypeStruct((b, t, h * LANES), F32),
                   jax.ShapeDtypeStruct((b, tc, h * LANES), F32)],
        scratch_shapes=[pltpu.VMEM((t, LANES), F32)] * 3 + [pltpu.VMEM((tc, LANES), F32)] * 3
                       + [pltpu.VMEM((t, LANES), F32), pltpu.VMEM((tc, LANES), F32),
                          pltpu.VMEM((2, n_tot, LANES, LANES), BF16), pltpu.VMEM((2, n_tot, LANES, LANES), F32),
                          pltpu.VMEM((2, n_tot, DN_CHUNK, LANES), BF16), pltpu.VMEM((2, n_tot, DN_CHUNK, LANES), F32),
                          pltpu.VMEM((2, n_tot, 8, LANES), F32)],
        compiler_params=_params("arbitrary", "arbitrary"),
        name="deltanet",
    )(*([pl_f32] * 5 + [pc_f32] * 5 + [conv_w, conv_w, conv_w, alog_rows, dtb_rows, ng.reshape(1, LANES)]))
    return yl, yc


_FF_CHUNKS = ((0, 1536), (1536, D_FF))


def _post_kernel(*refs, final):
    if final:
        x_ref, ya_ref, yb_ref, mod_ref, gf_ref, wa_ref, wb_ref, wg_ref, wu_ref, wd_ref, gfin_ref, o_ref = refs
    else:
        x_ref, ya_ref, yb_ref, mod_ref, gf_ref, wa_ref, wb_ref, wg_ref, wu_ref, wd_ref, o_ref = refs
    mod = mod_ref[...]
    ga1 = mod[:, 2 * D_MODEL:3 * D_MODEL]
    ga2 = mod[:, 5 * D_MODEL:6 * D_MODEL]
    ya = jnp.concatenate([_bf(ya_ref[g]) for g in range(ya_ref.shape[0])], axis=1)
    y = _dot(ya, wa_ref[...]) + _dot(_bf(yb_ref[...]), wb_ref[...])
    x1 = x_ref[...] + ga1 * y
    hb = _bf(_norm_mod(x1, gf_ref[...], mod, 3))
    acc = None
    for lo, hi in _FF_CHUNKS:
        gate = _dot(hb, wg_ref[:, lo:hi])
        up = _dot(hb, wu_ref[:, lo:hi])
        part = _dot(_bf(_silu(gate) * up), wd_ref[lo:hi, :])
        acc = part if acc is None else acc + part
    out = x1 + ga2 * acc
    if final:
        out = _rms(out, gfin_ref[...])
    o_ref[...] = out


def _post(x, ya, yb, mods, gf, wa, wb, wg, wu, wd, gfin, *, mod_row, tm):
    b, t, _ = x.shape
    rows = b * t
    nb = rows // tm
    final = gfin is not None
    ga_groups = ya.shape[1]
    na, nbw = ga_groups * LANES, yb.shape[-1]
    bpb = t // tm

    def const(shape):
        return pl.BlockSpec(shape, lambda i: (0, 0), pipeline_mode=pl.Buffered(1))
    in_specs = [pl.BlockSpec((tm, D_MODEL), lambda i: (i, 0)),
                pl.BlockSpec((None, ga_groups, tm, LANES), lambda i: (i // bpb, 0, i % bpb, 0)),
                pl.BlockSpec((tm, nbw), lambda i: (i, 0)),
                pl.BlockSpec((None, 1, 6 * D_MODEL), lambda i: (mod_row(i), 0, 0)),
                pl.BlockSpec((1, D_MODEL), lambda i: (0, 0)),
                const((na, D_MODEL)), const((nbw, D_MODEL)),
                const((D_MODEL, D_FF)), const((D_MODEL, D_FF)), const((D_FF, D_MODEL))]
    args = [x.reshape(rows, D_MODEL), ya, yb.reshape(rows, nbw), mods,
            gf.reshape(1, D_MODEL), wa, wb, wg, wu, wd]
    if final:
        in_specs.append(pl.BlockSpec((1, D_MODEL), lambda i: (0, 0)))
        args.append(gfin.reshape(1, D_MODEL))
    out = pl.pallas_call(
        functools.partial(_post_kernel, final=final),
        grid=(nb,),
        in_specs=in_specs,
        out_specs=pl.BlockSpec((tm, D_MODEL), lambda i: (i, 0)),
        out_shape=jax.ShapeDtypeStruct((rows, D_MODEL), F32),
        compiler_params=_params("arbitrary"),
        name="post_ffn",
    )(*args)
    return out.reshape(b, t, D_MODEL)


def _rope_tables(n_tok):
    rows = n_tok // GRID_W
    quarter = HD // 4
    freqs = ROPE_THETA ** (-jnp.arange(quarter, dtype=F32) / quarter)
    row = jnp.repeat(jnp.arange(rows, dtype=F32), GRID_W)
    col = jnp.tile(jnp.arange(GRID_W, dtype=F32), rows)
    ang = jnp.concatenate([row[:, None] * freqs, col[:, None] * freqs], axis=-1)
    cos, sin = jnp.cos(ang), jnp.sin(ang)
    lane = np.arange(LANES)
    d = lane % HD
    src = (d // 32) * quarter + (d % quarter)
    first = (d % 32) < quarter
    c = cos[:, src]
    s = sin[:, src]
    s1 = jnp.where(first[None, :], -s, 0.0)
    s2 = jnp.where(first[None, :], 0.0, s)
    return c, s1, s2


def _place(cols_src, width, dst_lanes):
    idx = np.zeros((width,), np.int32)
    msk = np.zeros((width,), bool)
    for src, dst in zip(cols_src, dst_lanes):
        idx[dst] = src
        msk[dst] = True
    return idx, msk


def _gather_cols(w, idx, msk):
    parts = []
    n = len(idx)
    i = 0
    while i < n:
        j = i
        if not msk[i]:
            while j < n and not msk[j]:
                j += 1
            parts.append(jnp.zeros((w.shape[0], j - i), w.dtype))
        else:
            while j + 1 < n and msk[j + 1] and idx[j + 1] == idx[j] + 1:
                j += 1
            j += 1
            parts.append(w[:, int(idx[i]):int(idx[i]) + (j - i)])
        i = j
    return jnp.concatenate(parts, axis=1)


def _even_layout():
    src, dst = [], []
    for i in range(3072):
        src.append(i)
        dst.append(i)
    gbase = 3072
    for p in range(2):
        for kind in range(4):
            for hh in range(2):
                src.append(gbase + kind * ML_HEADS + 2 * p + hh)
                dst.append(3072 + p * LANES + kind * 2 + hh)
    return _place(src, EVEN_BF + EVEN_F32, dst)


def _odd_layout():
    src, dst = [], []
    per_kv = GQ_HEADS // GQ_KV
    for h in range(GQ_HEADS):
        j = h // per_kv
        for e in range(HD):
            src.append(h * HD + e)
            dst.append(h * LANES + j * HD + e)
    for j in range(GQ_KV):
        for e in range(HD):
            src.append(512 + j * HD + e)
            dst.append(GQ_HEADS * LANES + j * LANES + j * HD + e)
    for e in range(GQ_KV * HD):
        src.append(640 + e)
        dst.append((GQ_HEADS + GQ_KV) * LANES + e)
    for e in range(1536 + 512):
        src.append(768 + e)
        dst.append(ODD_BF + e)
    for h in range(DN_HEADS):
        for d in range(2):
            for kind in range(2):
                src.append(2816 + d * 2 * DN_HEADS + kind * DN_HEADS + h)
                dst.append(ODD_BF + 2048 + h * LANES + d * 2 + kind)
    return _place(src, ODD_BF + ODD_F32, dst)


def kernel(x, c, ctx, c_ctx, w_ada, b_ada, g_mix, g_ffn, w_in_e, b_gate_e, da_lam, da_norm_g, ml_norm_g,
           w_out_e, w_in_o, qk_norm_g, dn_conv, dn_a_log, dn_dt_bias, dn_norm_g, w_out_o, w_gate, w_up,
           w_down, g_final):
    b, t, _ = x.shape
    tc = ctx.shape[1]
    tables = _rope_tables(t)

    cv = jnp.zeros((16, D_MODEL), F32).at[:b].set(c).at[b].set(c_ctx)
    mods = _ada_mods(cv, w_ada, b_ada).reshape(DEPTH, 16, 1, 6 * D_MODEL)

    e_idx, e_msk = _even_layout()
    o_idx, o_msk = _odd_layout()
    tm_l = 512
    tm_c = 256
    bpb = t // tm_l
    lat_row = lambda i: i // bpb
    ctx_row = lambda i: b

    xl, xc = x, ctx
    for layer in range(DEPTH):
        emit_ctx = layer < DEPTH - 1
        m = mods[layer]
        if layer % 2 == 0:
            e = layer // 2
            lam_init = 0.8 - 0.6 * math.exp(-0.3 * layer)
            w = _bf(_gather_cols(w_in_e[e], e_idx, e_msk))
            pl_bf, pl_f = _inproj(xl, m, g_mix[layer], w, [], tables, even=True, seq_len=t, mod_row=lat_row, tm=tm_l)
            pc_bf, pc_f = _inproj(xc, m, g_mix[layer], w, [], None, even=True, seq_len=tc, mod_row=ctx_row, tm=tm_c)
            ya_l = _diff_attn(pl_bf, [pc_bf, pl_bf], da_lam[e], da_norm_g[e], lam_init, 256, True)
            gbias = jnp.zeros((2, 1, LANES), F32)
            gb = b_gate_e[e].reshape(4, 2, 2)
            gbias = gbias.at[:, 0, :8].set(gb.transpose(1, 0, 2).reshape(2, 8))
            yb_l, yb_c = _mlstm(pl_f, pc_f, gbias, ml_norm_g[e])
            if emit_ctx:
                ya_c = _diff_attn(pc_bf, [pc_bf], da_lam[e], da_norm_g[e], lam_init, tc, False)
            w_out = _bf(w_out_e[e])
            wa, wb = w_out[:DA_HEADS * LANES], w_out[DA_HEADS * LANES:]
        else:
            o = layer // 2
            w = _bf(_gather_cols(w_in_o[o], o_idx, o_msk))
            qkg = jnp.zeros((2 * GQ_KV, LANES), F32)
            for j in range(GQ_KV):
                qkg = qkg.at[j, j * HD:(j + 1) * HD].set(qk_norm_g[o, 0])
                qkg = qkg.at[GQ_KV + j, j * HD:(j + 1) * HD].set(qk_norm_g[o, 1])
            pl_bf, pl_f = _inproj(xl, m, g_mix[layer], w, [qkg], tables, even=False, seq_len=t, mod_row=lat_row, tm=tm_l)
            pc_bf, pc_f = _inproj(xc, m, g_mix[layer], w, [qkg], None, even=False, seq_len=tc, mod_row=ctx_row, tm=tm_c)
            ya_l = _gqa_attn(pl_bf, [pc_bf, pl_bf], 256, True)
            alog = jnp.zeros((DN_HEADS, 1, LANES), F32)
            dtb = jnp.zeros((DN_HEADS, 1, LANES), F32)
            for d in range(2):
                alog = alog.at[:, 0, 2 * d + 1].set(dn_a_log[o, d])
                dtb = dtb.at[:, 0, 2 * d + 1].set(dn_dt_bias[o, d])
            yb_l, yb_c = _deltanet(pl_f, pc_f, dn_conv[o], alog, dtb, dn_norm_g[o])
            if emit_ctx:
                ya_c = _gqa_attn(pc_bf, [pc_bf], tc, False)
            w_out = _bf(w_out_o[o])
            wa, wb = w_out[:GQ_HEADS * HD], w_out[GQ_HEADS * HD:]
        wg, wu, wd = _bf(w_gate[layer]), _bf(w_up[layer]), _bf(w_down[layer])
        gfin = g_final if layer == DEPTH - 1 else None
        xl = _post(xl, ya_l, yb_l, m, g_ffn[layer], wa, wb, wg, wu, wd, gfin, mod_row=lat_row, tm=tm_l)
        if emit_ctx:
            xc = _post(xc, ya_c, yb_c, m, g_ffn[layer], wa, wb, wg, wu, wd, None, mod_row=ctx_row, tm=tm_c)
    return xl
```

```python
import functools
import math

import jax
import jax.numpy as jnp
import numpy as np
from jax import lax
from jax.experimental import pallas as pl
from jax.experimental.pallas import tpu as pltpu

F32 = jnp.float32
BF16 = jnp.bfloat16

D_MODEL = 1024
DEPTH = 4
GRID_W = 64
ROPE_THETA = 10000.0
EPS = 1e-6
HD = 64
Q_BLOCK = 128
LANES = 128
DA_HEADS = 4
ML_HEADS = 4
GQ_HEADS = 8
GQ_KV = 2
DN_HEADS = 4
D_FF = 2816
ML_CHUNK = 128
DN_CHUNK = 128
DN_GROUP = 8
VMEM_LIMIT = 56 * 1024 * 1024

EVEN_BF = 1536
EVEN_F32 = 1792
ODD_BF = 1408
ODD_F32 = 2560


def _bf(x):
    return x.astype(BF16)


def _dot(a, b):
    return jnp.dot(a, b, preferred_element_type=F32)


def _dot_nt(a, b):
    return lax.dot_general(a, b, (((1,), (1,)), ((), ())), preferred_element_type=F32)


def _dot_tn(a, b):
    return lax.dot_general(a, b, (((0,), (0,)), ((), ())), preferred_element_type=F32)


def _split2(x):
    hi = _bf(x)
    return hi, _bf(x - hi.astype(F32))


def _split3(x):
    hi = _bf(x)
    r = x - hi.astype(F32)
    mid = _bf(r)
    return hi, mid, _bf(r - mid.astype(F32))


def _dot_x3(a, b):
    ah, al = _split2(a)
    bh, bl = _split2(b)
    return _dot(ah, bh) + (_dot(ah, bl) + _dot(al, bh))


def _sel_dot(sel, x):
    h, m, l = _split3(x)
    return _dot(sel, h) + (_dot(sel, m) + _dot(sel, l))


def _sel_dot_nt(sel, x):
    h, m, l = _split3(x)
    return _dot_nt(sel, h) + (_dot_nt(sel, m) + _dot_nt(sel, l))


def _sigmoid(x):
    return 1.0 / (1.0 + jnp.exp(-x))


def _silu(x):
    return x * _sigmoid(x)


def _log_sigmoid(x):
    return jnp.minimum(x, 0.0) - jnp.log(1.0 + jnp.exp(-jnp.abs(x)))


def _softplus(x):
    return jnp.maximum(x, 0.0) + jnp.log(1.0 + jnp.exp(-jnp.abs(x)))


def _rms(x, g, n=None):
    n = x.shape[-1] if n is None else n
    ss = jnp.sum(x * x, axis=-1, keepdims=True)
    return x * lax.rsqrt(ss * (1.0 / n) + EPS) * g


def _norm_mod(x, g, mod, slot):
    sh = mod[:, slot * D_MODEL:(slot + 1) * D_MODEL]
    sc = mod[:, (slot + 1) * D_MODEL:(slot + 2) * D_MODEL]
    return _rms(x, g) * (1.0 + sc) + sh


def _rope(x, c, s1, s2):
    return x * c + pltpu.roll(x, LANES - 16, 1) * s1 + pltpu.roll(x, 16, 1) * s2


def _params(*sem):
    return pltpu.CompilerParams(dimension_semantics=sem, vmem_limit_bytes=VMEM_LIMIT)


def _ada_kernel(cv_ref, w_ref, b_ref, o_ref):
    cv = cv_ref[...]
    o_ref[...] = _dot_x3(_silu(cv), w_ref[...]) + b_ref[...]


def _ada_mods(cv, w_ada, b_ada):
    rows = cv.shape[0]
    nb = 6
    return pl.pallas_call(
        _ada_kernel,
        grid=(DEPTH, nb),
        in_specs=[pl.BlockSpec((rows, D_MODEL), lambda l, j: (0, 0)),
                  pl.BlockSpec((None, D_MODEL, D_MODEL), lambda l, j: (l, 0, j)),
                  pl.BlockSpec((None, 1, D_MODEL), lambda l, j: (l, 0, j))],
        out_specs=pl.BlockSpec((None, rows, D_MODEL), lambda l, j: (l, 0, j)),
        out_shape=jax.ShapeDtypeStruct((DEPTH, rows, 6 * D_MODEL), F32),
        compiler_params=_params("arbitrary", "arbitrary"),
        name="ada_mods",
    )(cv, w_ada, b_ada.reshape(DEPTH, 1, 6 * D_MODEL))


def _inproj_even_kernel(*refs, rope):
    if rope:
        x_ref, mod_ref, g_ref, w_ref, rc_ref, s1_ref, s2_ref, o1_ref, o2_ref = refs
    else:
        x_ref, mod_ref, g_ref, w_ref, o1_ref, o2_ref = refs
    hb = _bf(_norm_mod(x_ref[...], g_ref[...], mod_ref[...], 0))
    for seg in range(4):
        p = _dot(hb, w_ref[:, seg * 256:(seg + 1) * 256])
        for half in range(2):
            gi = seg * 2 + half
            xg = p[:, half * LANES:(half + 1) * LANES]
            if rope:
                xg = _rope(xg, rc_ref[...], s1_ref[...], s2_ref[...])
            if gi < DA_HEADS:
                xg = xg * (HD ** -0.5)
            o1_ref[:, gi * LANES:(gi + 1) * LANES] = _bf(xg)
    o1_ref[:, 1024:EVEN_BF] = _bf(_dot(hb, w_ref[:, 1024:EVEN_BF]))
    o2_ref[...] = _dot(hb, w_ref[:, EVEN_BF:])


def _inproj_odd_kernel(*refs, rope):
    if rope:
        x_ref, mod_ref, g_ref, w_ref, qkg_ref, rc_ref, s1_ref, s2_ref, o1_ref, o2_ref = refs
    else:
        x_ref, mod_ref, g_ref, w_ref, qkg_ref, o1_ref, o2_ref = refs
    hb = _bf(_norm_mod(x_ref[...], g_ref[...], mod_ref[...], 0))
    for seg in range(5):
        p = _dot(hb, w_ref[:, seg * 256:(seg + 1) * 256])
        for half in range(2):
            gi = seg * 2 + half
            xg = p[:, half * LANES:(half + 1) * LANES]
            if gi < GQ_HEADS:
                grow = gi // (GQ_HEADS // GQ_KV)
            else:
                grow = GQ_KV + (gi - GQ_HEADS)
            xg = _rms(xg, qkg_ref[grow:grow + 1, :], HD)
            if rope:
                xg = _rope(xg, rc_ref[...], s1_ref[...], s2_ref[...])
            if gi < GQ_HEADS:
                xg = xg * (HD ** -0.5)
            o1_ref[:, gi * LANES:(gi + 1) * LANES] = _bf(xg)
    o1_ref[:, 1280:ODD_BF] = _bf(_dot(hb, w_ref[:, 1280:ODD_BF]))
    o2_ref[...] = _dot(hb, w_ref[:, ODD_BF:])


def _inproj(x, mods, g, w, extra, tables, *, even, seq_len, mod_row, tm):
    b, t, _ = x.shape
    rows = b * t
    nb = rows // tm
    bpb = max(t // tm, 1)
    n_bf, n_f32 = (EVEN_BF, EVEN_F32) if even else (ODD_BF, ODD_F32)
    rope = tables is not None
    kern = functools.partial(_inproj_even_kernel if even else _inproj_odd_kernel, rope=rope)
    in_specs = [pl.BlockSpec((tm, D_MODEL), lambda i: (i, 0)),
                pl.BlockSpec((None, 1, 6 * D_MODEL), lambda i: (mod_row(i), 0, 0)),
                pl.BlockSpec((1, D_MODEL), lambda i: (0, 0)),
                pl.BlockSpec((D_MODEL, n_bf + n_f32), lambda i: (0, 0))]
    args = [x.reshape(rows, D_MODEL), mods, g.reshape(1, D_MODEL), w]
    for e in extra:
        in_specs.append(pl.BlockSpec(e.shape, lambda i: (0, 0)))
        args.append(e)
    if rope:
        for tab in tables:
            in_specs.append(pl.BlockSpec((tm, LANES), lambda i: (i % bpb, 0)))
            args.append(tab)
    o1, o2 = pl.pallas_call(
        kern,
        grid=(nb,),
        in_specs=in_specs,
        out_specs=[pl.BlockSpec((tm, n_bf), lambda i: (i, 0)),
                   pl.BlockSpec((tm, n_f32), lambda i: (i, 0))],
        out_shape=[jax.ShapeDtypeStruct((rows, n_bf), BF16),
                   jax.ShapeDtypeStruct((rows, n_f32), F32)],
        compiler_params=_params("arbitrary"),
        name="inproj_even" if even else "inproj_odd",
    )(*args)
    return o1.reshape(b, t, n_bf), o2.reshape(b, t, n_f32)


def _with_ones(v):
    return jnp.concatenate([v, jnp.ones_like(v)], axis=1)


def _softmax_pv(q, kvs):
    ss = [_dot_nt(q, k) for k, _ in kvs]
    mx = ss[0].max(axis=-1, keepdims=True)
    for s in ss[1:]:
        mx = jnp.maximum(mx, s.max(axis=-1, keepdims=True))
    acc = None
    for s, (_, v) in zip(ss, kvs):
        o = _dot(_bf(jnp.exp(s - mx)), v)
        acc = o if acc is None else acc + o
    return acc[:, :LANES] / acc[:, LANES:]


def _store_heads(o_ref, o, group, blocked):
    if not blocked:
        o_ref[group] = o
        return
    n_blocks = o_ref.shape[1] // Q_BLOCK
    per_step = o.shape[0] // Q_BLOCK
    for jj in range(per_step):
        j = pl.program_id(1) * per_step + jj
        o_ref[group, pl.ds(j, Q_BLOCK, stride=n_blocks), :] = o[jj * Q_BLOCK:(jj + 1) * Q_BLOCK, :]


def _attn_out(b, t, groups, tq, blocked):
    shape = jax.ShapeDtypeStruct((b, groups, t, LANES), F32)
    if blocked:
        return pl.BlockSpec((None, groups, t, LANES), lambda bi, qi: (bi, 0, 0, 0)), shape
    return pl.BlockSpec((None, groups, tq, LANES), lambda bi, qi: (bi, 0, qi, 0)), shape


def _diff_attn_kernel(*refs, n_kv, lam_init, blocked):
    q_ref = refs[0]
    kv_refs = refs[1:1 + 2 * n_kv]
    lam_ref, ng_ref, o_ref = refs[1 + 2 * n_kv:]
    lane = lax.broadcasted_iota(jnp.int32, (1, LANES), 1)
    lf = lam_ref[...]
    s1 = jnp.sum(lf[0:1, :] * lf[1:2, :], axis=-1, keepdims=True)
    s2 = jnp.sum(lf[2:3, :] * lf[3:4, :], axis=-1, keepdims=True)
    lmb = jnp.exp(s1) - jnp.exp(s2) + lam_init
    ng = ng_ref[...]
    for h in range(DA_HEADS):
        hs = slice(h * LANES, (h + 1) * LANES)
        q = q_ref[:, hs]
        kvs = [(kv_refs[2 * i][:, hs], _with_ones(kv_refs[2 * i + 1][:, hs])) for i in range(n_kv)]
        outs = []
        for m in range(2):
            in_map = (lane >= m * HD) & (lane < (m + 1) * HD)
            qm = jnp.where(in_map, q, jnp.zeros_like(q))
            outs.append(_softmax_pv(qm, kvs))
        o = outs[0] - lmb * outs[1]
        _store_heads(o_ref, _rms(o, ng) * (1.0 - lam_init), h, blocked)


def _diff_attn(q_src, kv_srcs, lam, ng, lam_init, tq, blocked):
    b, t, _ = q_src.shape
    nq = t // tq
    width = DA_HEADS * LANES
    in_specs = [pl.BlockSpec((None, tq, width), lambda bi, qi: (bi, qi, 0))]
    args = [q_src]
    for src in kv_srcs:
        n = src.shape[1]
        in_specs.append(pl.BlockSpec((None, n, width), lambda bi, qi: (bi, 0, 1)))
        in_specs.append(pl.BlockSpec((None, n, width), lambda bi, qi: (bi, 0, 2)))
        args += [src, src]
    in_specs += [pl.BlockSpec((4, HD), lambda bi, qi: (0, 0)),
                 pl.BlockSpec((1, LANES), lambda bi, qi: (0, 0))]
    args += [lam, ng.reshape(1, LANES)]
    out_spec, out_shape = _attn_out(b, t, DA_HEADS, tq, blocked)
    return pl.pallas_call(
        functools.partial(_diff_attn_kernel, n_kv=len(kv_srcs), lam_init=lam_init, blocked=blocked),
        grid=(b, nq),
        in_specs=in_specs,
        out_specs=out_spec,
        out_shape=out_shape,
        compiler_params=_params("arbitrary", "arbitrary"),
        name="diff_attn",
    )(*args)


def _gqa_kernel(*refs, n_kv, blocked):
    q_ref = refs[0]
    kv_refs = refs[1:1 + 2 * n_kv]
    o_ref = refs[1 + 2 * n_kv]
    lane = lax.broadcasted_iota(jnp.int32, (1, LANES), 1)
    low = lane < HD
    per_kv = GQ_HEADS // GQ_KV
    vals = [_with_ones(kv_refs[2 * i + 1][...]) for i in range(n_kv)]
    for pair in range(GQ_HEADS // 2):
        j = (2 * pair) // per_kv
        kvs = [(kv_refs[2 * i][:, j * LANES:(j + 1) * LANES], vals[i]) for i in range(n_kv)]
        o_a = _softmax_pv(q_ref[:, (2 * pair) * LANES:(2 * pair + 1) * LANES], kvs)
        o_b = _softmax_pv(q_ref[:, (2 * pair + 1) * LANES:(2 * pair + 2) * LANES], kvs)
        if j == 0:
            packed = jnp.where(low, o_a, pltpu.roll(o_b, HD, 1))
        else:
            packed = jnp.where(low, pltpu.roll(o_a, HD, 1), o_b)
        _store_heads(o_ref, packed, pair, blocked)


def _gqa_attn(q_src, kv_srcs, tq, blocked):
    b, t, _ = q_src.shape
    nq = t // tq
    qw = GQ_HEADS * LANES
    in_specs = [pl.BlockSpec((None, tq, qw), lambda bi, qi: (bi, qi, 0))]
    args = [q_src]
    for src in kv_srcs:
        n = src.shape[1]
        in_specs.append(pl.BlockSpec((None, n, GQ_KV * LANES), lambda bi, qi: (bi, 0, qw // (GQ_KV * LANES))))
        in_specs.append(pl.BlockSpec((None, n, LANES), lambda bi, qi: (bi, 0, (qw + GQ_KV * LANES) // LANES)))
        args += [src, src]
    out_spec, out_shape = _attn_out(b, t, GQ_HEADS // 2, tq, blocked)
    return pl.pallas_call(
        functools.partial(_gqa_kernel, n_kv=len(kv_srcs), blocked=blocked),
        grid=(b, nq),
        in_specs=in_specs,
        out_specs=out_spec,
        out_shape=out_shape,
        compiler_params=_params("arbitrary", "arbitrary"),
        name="gqa_attn",
    )(*args)


def _mlstm_kernel(ql_ref, kl_ref, vl_ref, ol_ref, gl_ref, qc_ref, kc_ref, vc_ref, oc_ref, gc_ref,
                  gb_ref, ng_ref, yl_ref, yc_ref, hbl_ref, hbc_ref, cst_ref, mst_ref, *, n_lat, n_ctx):
    c = ML_CHUNK
    lane = lax.broadcasted_iota(jnp.int32, (1, LANES), 1)
    ri = lax.broadcasted_iota(jnp.int32, (c, c), 0)
    ci = lax.broadcasted_iota(jnp.int32, (c, c), 1)
    masks = (ci <= ri, ci >= ri)
    cum_mat = _bf(masks[0].astype(F32))
    row_id = lax.broadcasted_iota(jnp.int32, (c, LANES), 0)
    ones_blk = jnp.ones((c, LANES), F32)
    gb = gb_ref[...]
    k3 = lax.broadcasted_iota(jnp.int32, (2 * LANES, LANES), 0) & (LANES - 1)
    c3 = lax.broadcasted_iota(jnp.int32, (c, 2 * LANES), 1) & (LANES - 1)
    units = [(d, hh) for d in range(2) for hh in range(2)]
    gate_lane = [d * 4 + hh for d, hh in units]
    sel_t = [_bf((k3 == cl).astype(F32)) for cl in gate_lane]
    sel_r = [_bf((c3 == cl).astype(F32)) for cl in gate_lane]
    head_mask = [(lane >= hh * HD) & (lane < (hh + 1) * HD) for hh in range(2)]

    cst_ref[...] = jnp.zeros_like(cst_ref)
    mst_ref[...] = jnp.zeros_like(mst_ref)

    def cummax_rows(x, reverse):
        k = 1
        while k < c:
            if reverse:
                sh = jnp.where(row_id < c - k, pltpu.roll(x, c - k, 0), -jnp.inf)
            else:
                sh = jnp.where(row_id >= k, pltpu.roll(x, k, 0), -jnp.inf)
            x = jnp.maximum(x, sh)
            k *= 2
        return x

    def gate_prep(g_ref, r0, d):
        g = g_ref[pl.ds(r0, c), :] + gb
        lf = _log_sigmoid(g)
        cum = _dot(cum_mat, jnp.concatenate(_split2(lf), axis=1))
        cum = cum[:, :LANES] + cum[:, LANES:]
        if d == 1:
            cum = jnp.sum(lf, axis=0, keepdims=True) - cum + lf
        bc = pltpu.roll(cum, LANES - 2, 1)
        w = g - bc
        cmx = cummax_rows(w, d == 1)
        edge = slice(c - 1, c) if d == 0 else slice(0, 1)
        h, l = _split2(jnp.concatenate([w, cmx, bc], axis=0))
        return jnp.concatenate([h, l], axis=1), bc[edge, :], cmx[edge, :]

    def chunk_step(q_ref, k_ref, v_ref, g_ref, dst_refs, rows):
        preps = [gate_prep(g_ref, rows[d], d) for d in range(2)]
        q2 = [q_ref[pl.ds(rows[d], c), :] for d in range(2)]
        k2 = [k_ref[pl.ds(rows[d], c), :] * (HD ** -0.5) for d in range(2)]
        v2 = [v_ref[pl.ds(rows[d], c), :] for d in range(2)]
        qm = [_bf(jnp.where(head_mask[hh], q2[d], 0.0)) for d, hh in units]
        km = [_bf(jnp.where(head_mask[hh], k2[d], 0.0)) for d, hh in units]
        vh = [v2[d][:, hh * LANES:(hh + 1) * LANES] for d, hh in units]
        bb = [_dot(preps[d][0], sel_t[i]) for i, (d, hh) in enumerate(units)]
        w_b = [x[0:c] for x in bb]
        cm_b = [x[c:2 * c] for x in bb]
        bc_b = [x[2 * c:3 * c] for x in bb]
        w_row = [_dot_nt(sel_r[i], preps[d][0][0:c]) for i, (d, hh) in enumerate(units)]
        qk = _each(_dot_nt, qm, km)
        p = [jnp.exp(jnp.where(masks[d], wr - cm, -jnp.inf)) * s
             for (d, hh), wr, cm, s in zip(units, w_row, cm_b, qk)]
        num_loc = [_dot(_bf(pi), _bf(jnp.concatenate([v, ones_blk], axis=1))) for pi, v in zip(p, vh)]

        m_st = [mst_ref[i][0:1, 0:1] for i in range(4)]
        c_st = [cst_ref[i] for i in range(4)]
        b_end = [preps[d][1][:, cl:cl + 1] for (d, hh), cl in zip(units, gate_lane)]
        w_max = [preps[d][2][:, cl:cl + 1] for (d, hh), cl in zip(units, gate_lane)]
        m_new = [jnp.maximum(be + ms, be + wm) for be, ms, wm in zip(b_end, m_st, w_max)]
        top = _each(jnp.maximum, m_st, cm_b)
        inter = [jnp.exp(ms - t) for ms, t in zip(m_st, top)]
        s_loc = [jnp.exp(cm - t) for cm, t in zip(cm_b, top)]
        floor = [jnp.exp(-(bc + t)) for bc, t in zip(bc_b, top)]
        q_c = [_dot(q, _bf(cs)) for q, cs in zip(qm, c_st)]
        for i, (d, hh) in enumerate(units):
            num = inter[i] * q_c[i][:, :LANES] + s_loc[i] * num_loc[i][:, :LANES]
            den = inter[i] * q_c[i][:, LANES:] + s_loc[i] * num_loc[i][:, LANES:]
            dst_refs[d][pl.ds(rows[d], c), hh * LANES:(hh + 1) * LANES] = num / jnp.maximum(jnp.abs(den), floor[i])
        w_in = [jnp.exp(be + wb - mn) for be, wb, mn in zip(b_end, w_b, m_new)]
        kv = [_dot_tn(k, _bf(jnp.concatenate([w * v, w], axis=1))) for k, w, v in zip(km, w_in, vh)]
        for i in range(4):
            cst_ref[i] = jnp.exp(b_end[i] + m_st[i] - m_new[i]) * c_st[i] + kv[i]
            mst_ref[i] = jnp.broadcast_to(m_new[i], mst_ref.shape[1:])

    for cc in range(n_ctx):
        chunk_step(qc_ref, kc_ref, vc_ref, gc_ref, (yc_ref, hbc_ref), (cc * c, (n_ctx - 1 - cc) * c))

    def body(cc, carry):
        rows = (pl.multiple_of(cc * c, c), pl.multiple_of((n_lat - 1 - cc) * c, c))
        chunk_step(ql_ref, kl_ref, vl_ref, gl_ref, (yl_ref, hbl_ref), rows)
        return carry

    lax.fori_loop(0, n_lat, body, 0, unroll=2)

    ng = ng_ref[...]

    def finish(y_ref, hb_ref, o_ref, n_rows):
        tile = 256
        for r in range(0, n_rows, tile):
            for hh in range(2):
                sl = (slice(r, r + tile), slice(hh * LANES, (hh + 1) * LANES))
                hsum = y_ref[sl] + hb_ref[sl]
                y_ref[sl] = _rms(hsum, ng) * _sigmoid(o_ref[sl])

    finish(yc_ref, hbc_ref, oc_ref, n_ctx * c)
    finish(yl_ref, hbl_ref, ol_ref, n_lat * c)


def _mlstm(pl_f32, pc_f32, gate_bias, ng):
    b, t, _ = pl_f32.shape
    tc = pc_f32.shape[1]
    n_pairs = ML_HEADS // 2
    def specs(n):
        return [pl.BlockSpec((None, n, LANES), lambda bi, p: (bi, 0, p)),
                pl.BlockSpec((None, n, LANES), lambda bi, p: (bi, 0, 2 + p)),
                pl.BlockSpec((None, n, 2 * LANES), lambda bi, p: (bi, 0, 2 + p)),
                pl.BlockSpec((None, n, 2 * LANES), lambda bi, p: (bi, 0, 4 + p)),
                pl.BlockSpec((None, n, LANES), lambda bi, p: (bi, 0, 12 + p))]
    in_specs = specs(t) + specs(tc) + [pl.BlockSpec((None, 1, LANES), lambda bi, p: (p, 0, 0)),
                                      pl.BlockSpec((1, LANES), lambda bi, p: (0, 0))]
    yl, yc = pl.pallas_call(
        functools.partial(_mlstm_kernel, n_lat=t // ML_CHUNK, n_ctx=tc // ML_CHUNK),
        grid=(b, n_pairs),
        in_specs=in_specs,
        out_specs=[pl.BlockSpec((None, t, 2 * LANES), lambda bi, p: (bi, 0, p)),
                   pl.BlockSpec((None, tc, 2 * LANES), lambda bi, p: (bi, 0, p))],
        out_shape=[jax.ShapeDtypeStruct((b, t, ML_HEADS * LANES), F32),
                   jax.ShapeDtypeStruct((b, tc, ML_HEADS * LANES), F32)],
        scratch_shapes=[pltpu.VMEM((t, 2 * LANES), F32), pltpu.VMEM((tc, 2 * LANES), F32),
                        pltpu.VMEM((4, LANES, 2 * LANES), F32), pltpu.VMEM((4, 8, LANES), F32)],
        compiler_params=_params("arbitrary", "arbitrary"),
        name="mlstm",
    )(*([pl_f32] * 5 + [pc_f32] * 5 + [gate_bias, ng.reshape(1, LANES)]))
    return yl, yc


def _each(f, *lists):
    return [f(*xs) for xs in zip(*lists)]


def _unit_tri_inverse(a, eye, ri, ci):
    def mm(u, w):
        return _dot(_bf(u), _bf(w))

    size = a[0].shape[0]
    inside = (ri >> 4) == (ci >> 4)
    d = [jnp.where(inside, ai, 0.0) for ai in a]
    x = [eye - di for di in d]
    p = _each(mm, d, d)
    for step in range(3):
        x = _each(lambda xi, ti: xi + ti, x, _each(mm, x, p))
        if step < 2:
            p = _each(mm, p, p)
    shift = 5
    while (1 << (shift - 1)) < size:
        merged = (ri >> shift) == (ci >> shift)
        e = [jnp.where(merged & jnp.logical_not(inside), ai, 0.0) for ai in a]
        x = _each(lambda xi, ti: xi - ti, x, _each(mm, x, _each(mm, e, x)))
        inside = merged
        shift += 1
    return x


def _dn_kernel(ql_ref, kl_ref, vl_ref, zl_ref, gl_ref, qc_ref, kc_ref, vc_ref, zc_ref, gc_ref,
               wq_ref, wk_ref, wv_ref, alog_ref, dtb_ref, ng_ref, yl_ref, yc_ref,
               sql_ref, skl_ref, svl_ref, sqc_ref, skc_ref, svc_ref, obl_ref, obc_ref,
               p_ref, n_ref, qt_ref, o_ref, eg_ref, *, n_lat, n_ctx):
    c = DN_CHUNK
    lane = lax.broadcasted_iota(jnp.int32, (1, LANES), 1)
    ri = lax.broadcasted_iota(jnp.int32, (c, c), 0)
    ci = lax.broadcasted_iota(jnp.int32, (c, c), 1)
    incl = (ci <= ri, ci >= ri)
    strict = (ci < ri, ci > ri)
    cums = tuple(_bf(m.astype(F32)) for m in incl)
    eye = (ci == ri).astype(F32)
    neg_a = -jnp.exp(alog_ref[...])
    dtb = dtb_ref[...]

    def conv_prep(x_ref, w_ref, dst_ref, kind):
        n = x_ref.shape[0]
        x = x_ref[...]
        rows = lax.broadcasted_iota(jnp.int32, (n, LANES), 0)
        prev = jnp.where(rows == 0, 0.0, pltpu.roll(x, 1, 0))
        nxt = jnp.where(rows == n - 1, 0.0, pltpu.roll(x, n - 1, 0))
        w = w_ref[...]
        y = _silu(prev * w[0:1, :] + x * w[1:2, :] + nxt * w[2:3, :])
        if kind != "v":
            y = y * lax.rsqrt(jnp.sum(y * y, axis=-1, keepdims=True) + EPS)
        if kind == "q":
            y = y * (LANES ** -0.5)
        dst_ref[...] = y

    conv_prep(qc_ref, wq_ref, sqc_ref, "q")
    conv_prep(kc_ref, wk_ref, skc_ref, "k")
    conv_prep(vc_ref, wv_ref, svc_ref, "v")
    conv_prep(ql_ref, wq_ref, sql_ref, "q")
    conv_prep(kl_ref, wk_ref, skl_ref, "k")
    conv_prep(vl_ref, wv_ref, svl_ref, "v")
    lane2 = lax.broadcasted_iota(jnp.int32, (c, 2 * LANES), 1) & (LANES - 1)
    sels = tuple(_bf((lane2 == 2 * d + 1).astype(F32)) for d in range(2))

    def prep_group(q_ref, k_ref, v_ref, g_ref, c0, n_grp, n_seq, base):
        chunks = range(n_grp)
        units = [(u, d) for u in chunks for d in range(2)]
        rows = [pl.multiple_of((c0 + u) * c, c) for u in chunks]
        q = [q_ref[pl.ds(r, c), :] for r in rows]
        k = [k_ref[pl.ds(r, c), :] for r in rows]
        v = [v_ref[pl.ds(r, c), :] for r in rows]
        g = [g_ref[pl.ds(r, c), :] for r in rows]
        qb = [_bf(x) for x in q]
        kb = [_bf(x) for x in k]
        kk = _each(_dot_nt, kb, kb)
        qk = _each(_dot_nt, qb, kb)
        sig = [_sigmoid(x) for x in g]
        gdec = [neg_a * _softplus(x + dtb) for x in g]
        tot = [jnp.sum(x, axis=0, keepdims=True) for x in gdec]
        cum_f = [_dot(cums[0], jnp.concatenate(_split2(x), axis=1)) for x in gdec]
        cum_f = [x[:, :LANES] + x[:, LANES:] for x in cum_f]
        cum = [jnp.where(lane < 2, cf, t - cf + x) for cf, t, x in zip(cum_f, tot, gdec)]
        gcum = [cum[u][:, 2 * d + 1:2 * d + 2] for u, d in units]
        g_end = [tot[u][:, 2 * d + 1:2 * d + 2] for u, d in units]
        beta = [sig[u][:, 2 * d:2 * d + 1] for u, d in units]
        cum_hl = [jnp.concatenate(_split2(x), axis=1) for x in cum]
        g_row = [_dot_nt(sels[d], cum_hl[u]) for u, d in units]
        decay = [jnp.exp(jnp.where(incl[d], gc - gr, -jnp.inf)) for (u, d), gc, gr in zip(units, gcum, g_row)]
        a = [jnp.where(strict[d], bt * dc * kk[u], 0.0) for (u, d), bt, dc in zip(units, beta, decay)]
        tinv = _unit_tri_inverse(a, eye, ri, ci)
        gam = [jnp.exp(x) for x in gcum]
        rhs = [jnp.concatenate([bt * v[u], (bt * gm) * k[u]], axis=1) for (u, d), bt, gm in zip(units, beta, gam)]
        sol = [_bf(_dot(_bf(ti), _bf(r))) for ti, r in zip(tinv, rhs)]
        attn = [_bf(qk[u] * dc) for (u, d), dc in zip(units, decay)]
        kdec = [_bf(k[u] * jnp.exp(ge - gc)) for (u, d), ge, gc in zip(units, g_end, gcum)]
        qo = _each(_dot, attn, sol)
        pn = _each(_dot_tn, kdec, sol)
        for i, (u, d) in enumerate(units):
            chunk = c0 + u
            pos = base + (chunk if d == 0 else n_seq - 1 - chunk)
            p_ref[d, pos] = _bf(pn[i][:, LANES:])
            n_ref[d, pos] = pn[i][:, :LANES]
            qt_ref[d, pos] = _bf(gam[i] * q[u] - qo[i][:, LANES:])
            o_ref[d, pos] = qo[i][:, :LANES]
            eg_ref[d, pos] = jnp.broadcast_to(jnp.exp(g_end[i]), (8, LANES))

    grp_c = math.gcd(n_ctx, DN_GROUP)
    grp_l = math.gcd(n_lat, DN_GROUP)
    for grp in range(n_ctx // grp_c):
        prep_group(sqc_ref, skc_ref, svc_ref, gc_ref, grp * grp_c, grp_c, n_ctx, 0)

    def prep_body(grp, carry):
        prep_group(sql_ref, skl_ref, svl_ref, gl_ref, grp * grp_l, grp_l, n_lat, n_ctx)
        return carry

    lax.fori_loop(0, n_lat // grp_l, prep_body, 0)

    def scan_body(dst_refs, n_seq, base):
        def body(i, states):
            pos = base + i
            rows = (pl.multiple_of(i * c, c), pl.multiple_of((n_seq - 1 - i) * c, c))
            new = []
            for d in range(2):
                s = states[d]
                sb = _bf(s)
                dst_refs[d][pl.ds(rows[d], c), :] = _dot(qt_ref[d, pos], sb) + o_ref[d, pos]
                new.append(eg_ref[d, pos][0:1, :] * s - _dot(p_ref[d, pos], sb) + n_ref[d, pos])
            return tuple(new)
        return body

    zero = jnp.zeros((LANES, LANES), F32)
    states = lax.fori_loop(0, n_ctx, scan_body((yc_ref, obc_ref), n_ctx, 0), (zero, zero))
    lax.fori_loop(0, n_lat, scan_body((yl_ref, obl_ref), n_lat, n_ctx), states, unroll=4)

    ng = ng_ref[...]

    def finish(y_ref, ob_ref, z_ref):
        n_rows = y_ref.shape[0]
        tile = 256
        for r in range(0, n_rows, tile):
            sl = slice(r, r + tile)
            y_ref[sl, :] = _rms(y_ref[sl, :] + ob_ref[sl, :], ng) * _silu(z_ref[sl, :])

    finish(yc_ref, obc_ref, zc_ref)
    finish(yl_ref, obl_ref, zl_ref)


def _deltanet(pl_f32, pc_f32, conv_w, alog_rows, dtb_rows, ng):
    b, t, _ = pl_f32.shape
    tc = pc_f32.shape[1]
    h = DN_HEADS
    n_tot = (t + tc) // DN_CHUNK

    def specs(n):
        return [pl.BlockSpec((None, n, LANES), lambda bi, hi: (bi, 0, hi)),
                pl.BlockSpec((None, n, LANES), lambda bi, hi: (bi, 0, h + hi)),
                pl.BlockSpec((None, n, LANES), lambda bi, hi: (bi, 0, 2 * h + hi)),
                pl.BlockSpec((None, n, LANES), lambda bi, hi: (bi, 0, 3 * h + hi)),
                pl.BlockSpec((None, n, LANES), lambda bi, hi: (bi, 0, 4 * h + hi))]
    in_specs = specs(t) + specs(tc) + [
        pl.BlockSpec((3, LANES), lambda bi, hi: (0, hi)),
        pl.BlockSpec((3, LANES), lambda bi, hi: (0, h + hi)),
        pl.BlockSpec((3, LANES), lambda bi, hi: (0, 2 * h + hi)),
        pl.BlockSpec((None, 1, LANES), lambda bi, hi: (hi, 0, 0)),
        pl.BlockSpec((None, 1, LANES), lambda bi, hi: (hi, 0, 0)),
        pl.BlockSpec((1, LANES), lambda bi, hi: (0, 0))]
    yl, yc = pl.pallas_call(
        functools.partial(_dn_kernel, n_lat=t // DN_CHUNK, n_ctx=tc // DN_CHUNK),
        grid=(b, h),
        in_specs=in_specs,
        out_specs=[pl.BlockSpec((None, t, LANES), lambda bi, hi: (bi, 0, hi)),
                   pl.BlockSpec((None, tc, LANES), lambda bi, hi: (bi, 0, hi))],
        out_shape=[jax.ShapeDtypeStruct((b, t, h * LANES), F32),
                   jax.ShapeDtypeStruct((b, tc, h * LANES), F32)],
        scratch_shapes=[pltpu.VMEM((t, LANES), F32)] * 3 + [pltpu.VMEM((tc, LANES), F32)] * 3
                       + [pltpu.VMEM((t, LANES), F32), pltpu.VMEM((tc, LANES), F32),
                          pltpu.VMEM((2, n_tot, LANES, LANES), BF16), pltpu.VMEM((2, n_tot, LANES, LANES), F32),
                          pltpu.VMEM((2, n_tot, DN_CHUNK, LANES), BF16), pltpu.VMEM((2, n_tot, DN_CHUNK, LANES), F32),
                          pltpu.VMEM((2, n_tot, 8, LANES), F32)],
        compiler_params=_params("arbitrary", "arbitrary"),
        name="deltanet",
    )(*([pl_f32] * 5 + [pc_f32] * 5 + [conv_w, conv_w, conv_w, alog_rows, dtb_rows, ng.reshape(1, LANES)]))
    return yl, yc


_FF_CHUNKS = ((0, 1536), (1536, D_FF))


def _post_kernel(*refs, final):
    if final:
        x_ref, ya_ref, yb_ref, mod_ref, gf_ref, wa_ref, wb_ref, wg_ref, wu_ref, wd_ref, gfin_ref, o_ref = refs
    else:
        x_ref, ya_ref, yb_ref, mod_ref, gf_ref, wa_ref, wb_ref, wg_ref, wu_ref, wd_ref, o_ref = refs
    mod = mod_ref[...]
    ga1 = mod[:, 2 * D_MODEL:3 * D_MODEL]
    ga2 = mod[:, 5 * D_MODEL:6 * D_MODEL]
    ya = jnp.concatenate([_bf(ya_ref[g]) for g in range(ya_ref.shape[0])], axis=1)
    y = _dot(ya, wa_ref[...]) + _dot(_bf(yb_ref[...]), wb_ref[...])
    x1 = x_ref[...] + ga1 * y
    hb = _bf(_norm_mod(x1, gf_ref[...], mod, 3))
    acc = None
    for lo, hi in _FF_CHUNKS:
        gate = _dot(hb, wg_ref[:, lo:hi])
        up = _dot(hb, wu_ref[:, lo:hi])
        part = _dot(_bf(_silu(gate) * up), wd_ref[lo:hi, :])
        acc = part if acc is None else acc + part
    out = x1 + ga2 * acc
    if final:
        out = _rms(out, gfin_ref[...])
    o_ref[...] = out


def _post(x, ya, yb, mods, gf, wa, wb, wg, wu, wd, gfin, *, mod_row, tm):
    b, t, _ = x.shape
    rows = b * t
    nb = rows // tm
    final = gfin is not None
    ga_groups = ya.shape[1]
    na, nbw = ga_groups * LANES, yb.shape[-1]
    bpb = t // tm

    def const(shape):
        return pl.BlockSpec(shape, lambda i: (0, 0), pipeline_mode=pl.Buffered(1))
    in_specs = [pl.BlockSpec((tm, D_MODEL), lambda i: (i, 0)),
                pl.BlockSpec((None, ga_groups, tm, LANES), lambda i: (i // bpb, 0, i % bpb, 0)),
                pl.BlockSpec((tm, nbw), lambda i: (i, 0)),
                pl.BlockSpec((None, 1, 6 * D_MODEL), lambda i: (mod_row(i), 0, 0)),
                pl.BlockSpec((1, D_MODEL), lambda i: (0, 0)),
                const((na, D_MODEL)), const((nbw, D_MODEL)),
                const((D_MODEL, D_FF)), const((D_MODEL, D_FF)), const((D_FF, D_MODEL))]
    args = [x.reshape(rows, D_MODEL), ya, yb.reshape(rows, nbw), mods,
            gf.reshape(1, D_MODEL), wa, wb, wg, wu, wd]
    if final:
        in_specs.append(pl.BlockSpec((1, D_MODEL), lambda i: (0, 0)))
        args.append(gfin.reshape(1, D_MODEL))
    out = pl.pallas_call(
        functools.partial(_post_kernel, final=final),
        grid=(nb,),
        in_specs=in_specs,
        out_specs=pl.BlockSpec((tm, D_MODEL), lambda i: (i, 0)),
        out_shape=jax.ShapeDtypeStruct((rows, D_MODEL), F32),
        compiler_params=_params("arbitrary"),
        name="post_ffn",
    )(*args)
    return out.reshape(b, t, D_MODEL)


def _rope_tables(n_tok):
    rows = n_tok // GRID_W
    quarter = HD // 4
    freqs = ROPE_THETA ** (-jnp.arange(quarter, dtype=F32) / quarter)
    row = jnp.repeat(jnp.arange(rows, dtype=F32), GRID_W)
    col = jnp.tile(jnp.arange(GRID_W, dtype=F32), rows)
    ang = jnp.concatenate([row[:, None] * freqs, col[:, None] * freqs], axis=-1)
    cos, sin = jnp.cos(ang), jnp.sin(ang)
    lane = np.arange(LANES)
    d = lane % HD
    src = (d // 32) * quarter + (d % quarter)
    first = (d % 32) < quarter
    c = cos[:, src]
    s = sin[:, src]
    s1 = jnp.where(first[None, :], -s, 0.0)
    s2 = jnp.where(first[None, :], 0.0, s)
    return c, s1, s2


def _place(cols_src, width, dst_lanes):
    idx = np.zeros((width,), np.int32)
    msk = np.zeros((width,), bool)
    for src, dst in zip(cols_src, dst_lanes):
        idx[dst] = src
        msk[dst] = True
    return idx, msk


def _gather_cols(w, idx, msk):
    parts = []
    n = len(idx)
    i = 0
    while i < n:
        j = i
        if not msk[i]:
            while j < n and not msk[j]:
                j += 1
            parts.append(jnp.zeros((w.shape[0], j - i), w.dtype))
        else:
            while j + 1 < n and msk[j + 1] and idx[j + 1] == idx[j] + 1:
                j += 1
            j += 1
            parts.append(w[:, int(idx[i]):int(idx[i]) + (j - i)])
        i = j
    return jnp.concatenate(parts, axis=1)


def _even_layout():
    src, dst = [], []
    for i in range(3072):
        src.append(i)
        dst.append(i)
    gbase = 3072
    for p in range(2):
        for kind in range(4):
            for hh in range(2):
                src.append(gbase + kind * ML_HEADS + 2 * p + hh)
                dst.append(3072 + p * LANES + kind * 2 + hh)
    return _place(src, EVEN_BF + EVEN_F32, dst)


def _odd_layout():
    src, dst = [], []
    per_kv = GQ_HEADS // GQ_KV
    for h in range(GQ_HEADS):
        j = h // per_kv
        for e in range(HD):
            src.append(h * HD + e)
            dst.append(h * LANES + j * HD + e)
    for j in range(GQ_KV):
        for e in range(HD):
            src.append(512 + j * HD + e)
            dst.append(GQ_HEADS * LANES + j * LANES + j * HD + e)
    for e in range(GQ_KV * HD):
        src.append(640 + e)
        dst.append((GQ_HEADS + GQ_KV) * LANES + e)
    for e in range(1536 + 512):
        src.append(768 + e)
        dst.append(ODD_BF + e)
    for h in range(DN_HEADS):
        for d in range(2):
            for kind in range(2):
                src.append(2816 + d * 2 * DN_HEADS + kind * DN_HEADS + h)
                dst.append(ODD_BF + 2048 + h * LANES + d * 2 + kind)
    return _place(src, ODD_BF + ODD_F32, dst)


def kernel(x, c, ctx, c_ctx, w_ada, b_ada, g_mix, g_ffn, w_in_e, b_gate_e, da_lam, da_norm_g, ml_norm_g,
           w_out_e, w_in_o, qk_norm_g, dn_conv, dn_a_log, dn_dt_bias, dn_norm_g, w_out_o, w_gate, w_up,
           w_down, g_final):
    b, t, _ = x.shape
    tc = ctx.shape[1]
    tables = _rope_tables(t)

    cv = jnp.zeros((16, D_MODEL), F32).at[:b].set(c).at[b].set(c_ctx)
    mods = _ada_mods(cv, w_ada, b_ada).reshape(DEPTH, 16, 1, 6 * D_MODEL)

    e_idx, e_msk = _even_layout()
    o_idx, o_msk = _odd_layout()
    tm_l = 512
    tm_c = 256
    bpb = t // tm_l
    lat_row = lambda i: i // bpb
    ctx_row = lambda i: b

    xl, xc = x, ctx
    for layer in range(DEPTH):
        emit_ctx = layer < DEPTH - 1
        m = mods[layer]
        if layer % 2 == 0:
            e = layer // 2
            lam_init = 0.8 - 0.6 * math.exp(-0.3 * layer)
            w = _bf(_gather_cols(w_in_e[e], e_idx, e_msk))
            pl_bf, pl_f = _inproj(xl, m, g_mix[layer], w, [], tables, even=True, seq_len=t, mod_row=lat_row, tm=tm_l)
            pc_bf, pc_f = _inproj(xc, m, g_mix[layer], w, [], None, even=True, seq_len=tc, mod_row=ctx_row, tm=tm_c)
            ya_l = _diff_attn(pl_bf, [pc_bf, pl_bf], da_lam[e], da_norm_g[e], lam_init, 512, True)
            gbias = jnp.zeros((2, 1, LANES), F32)
            gb = b_gate_e[e].reshape(4, 2, 2)
            gbias = gbias.at[:, 0, :8].set(gb.transpose(1, 0, 2).reshape(2, 8))
            yb_l, yb_c = _mlstm(pl_f, pc_f, gbias, ml_norm_g[e])
            if emit_ctx:
                ya_c = _diff_attn(pc_bf, [pc_bf], da_lam[e], da_norm_g[e], lam_init, tc, False)
            w_out = _bf(w_out_e[e])
            wa, wb = w_out[:DA_HEADS * LANES], w_out[DA_HEADS * LANES:]
        else:
            o = layer // 2
            w = _bf(_gather_cols(w_in_o[o], o_idx, o_msk))
            qkg = jnp.zeros((2 * GQ_KV, LANES), F32)
            for j in range(GQ_KV):
                qkg = qkg.at[j, j * HD:(j + 1) * HD].set(qk_norm_g[o, 0])
                qkg = qkg.at[GQ_KV + j, j * HD:(j + 1) * HD].set(qk_norm_g[o, 1])
            pl_bf, pl_f = _inproj(xl, m, g_mix[layer], w, [qkg], tables, even=False, seq_len=t, mod_row=lat_row, tm=tm_l)
            pc_bf, pc_f = _inproj(xc, m, g_mix[layer], w, [qkg], None, even=False, seq_len=tc, mod_row=ctx_row, tm=tm_c)
            ya_l = _gqa_attn(pl_bf, [pc_bf, pl_bf], 512, True)
            alog = jnp.zeros((DN_HEADS, 1, LANES), F32)
            dtb = jnp.zeros((DN_HEADS, 1, LANES), F32)
            for d in range(2):
                alog = alog.at[:, 0, 2 * d + 1].set(dn_a_log[o, d])
                dtb = dtb.at[:, 0, 2 * d + 1].set(dn_dt_bias[o, d])
            yb_l, yb_c = _deltanet(pl_f, pc_f, dn_conv[o], alog, dtb, dn_norm_g[o])
            if emit_ctx:
                ya_c = _gqa_attn(pc_bf, [pc_bf], tc, False)
            w_out = _bf(w_out_o[o])
            wa, wb = w_out[:GQ_HEADS * HD], w_out[GQ_HEADS * HD:]
        wg, wu, wd = _bf(w_gate[layer]), _bf(w_up[layer]), _bf(w_down[layer])
        gfin = g_final if layer == DEPTH - 1 else None
        xl = _post(xl, ya_l, yb_l, m, g_ffn[layer], wa, wb, wg, wu, wd, gfin, mod_row=lat_row, tm=tm_l)
        if emit_ctx:
            xc = _post(xc, ya_c, yb_c, m, g_ffn[layer], wa, wb, wg, wu, wd, None, mod_row=ctx_row, tm=tm_c)
    return xl
```

```python
import functools
import math

import jax
import jax.numpy as jnp
import numpy as np
from jax import lax
from jax.experimental import pallas as pl
from jax.experimental.pallas import tpu as pltpu

F32 = jnp.float32
BF16 = jnp.bfloat16

D_MODEL = 1024
DEPTH = 4
GRID_W = 64
ROPE_THETA = 10000.0
EPS = 1e-6
HD = 64
Q_BLOCK = 128
LANES = 128
DA_HEADS = 4
ML_HEADS = 4
GQ_HEADS = 8
GQ_KV = 2
DN_HEADS = 4
D_FF = 2816
ML_CHUNK = 128
DN_CHUNK = 128
DN_GROUP = 8
VMEM_LIMIT = 56 * 1024 * 1024

EVEN_BF = 1536
EVEN_F32 = 1792
ODD_BF = 1408
ODD_F32 = 2560


def _bf(x):
    return x.astype(BF16)


def _dot(a, b):
    return jnp.dot(a, b, preferred_element_type=F32)


def _dot_nt(a, b):
    return lax.dot_general(a, b, (((1,), (1,)), ((), ())), preferred_element_type=F32)


def _dot_tn(a, b):
    return lax.dot_general(a, b, (((0,), (0,)), ((), ())), preferred_element_type=F32)


def _split2(x):
    hi = _bf(x)
    return hi, _bf(x - hi.astype(F32))


def _split3(x):
    hi = _bf(x)
    r = x - hi.astype(F32)
    mid = _bf(r)
    return hi, mid, _bf(r - mid.astype(F32))


def _dot_x3(a, b):
    ah, al = _split2(a)
    bh, bl = _split2(b)
    return _dot(ah, bh) + (_dot(ah, bl) + _dot(al, bh))


def _sel_dot(sel, x):
    h, m, l = _split3(x)
    return _dot(sel, h) + (_dot(sel, m) + _dot(sel, l))


def _sel_dot_nt(sel, x):
    h, m, l = _split3(x)
    return _dot_nt(sel, h) + (_dot_nt(sel, m) + _dot_nt(sel, l))


def _sigmoid(x):
    return 1.0 / (1.0 + jnp.exp(-x))


def _silu(x):
    return x * _sigmoid(x)


def _log_sigmoid(x):
    return jnp.minimum(x, 0.0) - jnp.log(1.0 + jnp.exp(-jnp.abs(x)))


def _softplus(x):
    return jnp.maximum(x, 0.0) + jnp.log(1.0 + jnp.exp(-jnp.abs(x)))


def _rms(x, g, n=None):
    n = x.shape[-1] if n is None else n
    ss = jnp.sum(x * x, axis=-1, keepdims=True)
    return x * lax.rsqrt(ss * (1.0 / n) + EPS) * g


def _norm_mod(x, g, mod, slot):
    sh = mod[:, slot * D_MODEL:(slot + 1) * D_MODEL]
    sc = mod[:, (slot + 1) * D_MODEL:(slot + 2) * D_MODEL]
    return _rms(x, g) * (1.0 + sc) + sh


def _rope(x, c, s1, s2):
    return x * c + pltpu.roll(x, LANES - 16, 1) * s1 + pltpu.roll(x, 16, 1) * s2


def _params(*sem):
    return pltpu.CompilerParams(dimension_semantics=sem, vmem_limit_bytes=VMEM_LIMIT)


def _ada_kernel(cv_ref, w_ref, b_ref, o_ref):
    cv = cv_ref[...]
    o_ref[...] = _dot_x3(_silu(cv), w_ref[...]) + b_ref[...]


def _ada_mods(cv, w_ada, b_ada):
    rows = cv.shape[0]
    nb = 6
    return pl.pallas_call(
        _ada_kernel,
        grid=(DEPTH, nb),
        in_specs=[pl.BlockSpec((rows, D_MODEL), lambda l, j: (0, 0)),
                  pl.BlockSpec((None, D_MODEL, D_MODEL), lambda l, j: (l, 0, j)),
                  pl.BlockSpec((None, 1, D_MODEL), lambda l, j: (l, 0, j))],
        out_specs=pl.BlockSpec((None, rows, D_MODEL), lambda l, j: (l, 0, j)),
        out_shape=jax.ShapeDtypeStruct((DEPTH, rows, 6 * D_MODEL), F32),
        compiler_params=_params("arbitrary", "arbitrary"),
        name="ada_mods",
    )(cv, w_ada, b_ada.reshape(DEPTH, 1, 6 * D_MODEL))


def _inproj_even_kernel(*refs, rope):
    if rope:
        x_ref, mod_ref, g_ref, w_ref, rc_ref, s1_ref, s2_ref, o1_ref, o2_ref = refs
    else:
        x_ref, mod_ref, g_ref, w_ref, o1_ref, o2_ref = refs
    hb = _bf(_norm_mod(x_ref[...], g_ref[...], mod_ref[...], 0))
    for seg in range(4):
        p = _dot(hb, w_ref[:, seg * 256:(seg + 1) * 256])
        for half in range(2):
            gi = seg * 2 + half
            xg = p[:, half * LANES:(half + 1) * LANES]
            if rope:
                xg = _rope(xg, rc_ref[...], s1_ref[...], s2_ref[...])
            if gi < DA_HEADS:
                xg = xg * (HD ** -0.5)
            o1_ref[:, gi * LANES:(gi + 1) * LANES] = _bf(xg)
    o1_ref[:, 1024:EVEN_BF] = _bf(_dot(hb, w_ref[:, 1024:EVEN_BF]))
    o2_ref[...] = _dot(hb, w_ref[:, EVEN_BF:])


def _inproj_odd_kernel(*refs, rope):
    if rope:
        x_ref, mod_ref, g_ref, w_ref, qkg_ref, rc_ref, s1_ref, s2_ref, o1_ref, o2_ref = refs
    else:
        x_ref, mod_ref, g_ref, w_ref, qkg_ref, o1_ref, o2_ref = refs
    hb = _bf(_norm_mod(x_ref[...], g_ref[...], mod_ref[...], 0))
    for seg in range(5):
        p = _dot(hb, w_ref[:, seg * 256:(seg + 1) * 256])
        for half in range(2):
            gi = seg * 2 + half
            xg = p[:, half * LANES:(half + 1) * LANES]
            if gi < GQ_HEADS:
                grow = gi // (GQ_HEADS // GQ_KV)
            else:
                grow = GQ_KV + (gi - GQ_HEADS)
            xg = _rms(xg, qkg_ref[grow:grow + 1, :], HD)
            if rope:
                xg = _rope(xg, rc_ref[...], s1_ref[...], s2_ref[...])
            if gi < GQ_HEADS:
                xg = xg * (HD ** -0.5)
            o1_ref[:, gi * LANES:(gi + 1) * LANES] = _bf(xg)
    o1_ref[:, 1280:ODD_BF] = _bf(_dot(hb, w_ref[:, 1280:ODD_BF]))
    o2_ref[...] = _dot(hb, w_ref[:, ODD_BF:])


def _inproj(x, mods, g, w, widx, extra, tables, *, even, seq_len, mod_row, tm):
    b, t, _ = x.shape
    rows = b * t
    nb = rows // tm
    bpb = max(t // tm, 1)
    n_bf, n_f32 = (EVEN_BF, EVEN_F32) if even else (ODD_BF, ODD_F32)
    rope = tables is not None
    kern = functools.partial(_inproj_even_kernel if even else _inproj_odd_kernel, rope=rope)
    in_specs = [pl.BlockSpec((tm, D_MODEL), lambda i: (i, 0)),
                pl.BlockSpec((None, 1, 6 * D_MODEL), lambda i: (mod_row(i), 0, 0)),
                pl.BlockSpec((1, D_MODEL), lambda i: (0, 0)),
                pl.BlockSpec((None, D_MODEL, n_bf + n_f32), lambda i: (widx, 0, 0))]
    args = [x.reshape(rows, D_MODEL), mods, g.reshape(1, D_MODEL), w]
    for e in extra:
        in_specs.append(pl.BlockSpec(e.shape, lambda i: (0, 0)))
        args.append(e)
    if rope:
        for tab in tables:
            in_specs.append(pl.BlockSpec((tm, LANES), lambda i: (i % bpb, 0)))
            args.append(tab)
    o1, o2 = pl.pallas_call(
        kern,
        grid=(nb,),
        in_specs=in_specs,
        out_specs=[pl.BlockSpec((tm, n_bf), lambda i: (i, 0)),
                   pl.BlockSpec((tm, n_f32), lambda i: (i, 0))],
        out_shape=[jax.ShapeDtypeStruct((rows, n_bf), BF16),
                   jax.ShapeDtypeStruct((rows, n_f32), F32)],
        compiler_params=_params("arbitrary"),
        name="inproj_even" if even else "inproj_odd",
    )(*args)
    return o1.reshape(b, t, n_bf), o2.reshape(b, t, n_f32)


def _with_ones(v):
    return jnp.concatenate([v, jnp.ones_like(v)], axis=1)


def _softmax_pv(q, kvs):
    ss = [_dot_nt(q, k) for k, _ in kvs]
    mx = ss[0].max(axis=-1, keepdims=True)
    for s in ss[1:]:
        mx = jnp.maximum(mx, s.max(axis=-1, keepdims=True))
    acc = None
    for s, (_, v) in zip(ss, kvs):
        o = _dot(_bf(jnp.exp(s - mx)), v)
        acc = o if acc is None else acc + o
    return acc[:, :LANES] / acc[:, LANES:]


def _store_heads(o_ref, o, group, blocked):
    if not blocked:
        o_ref[group] = o
        return
    n_blocks = o_ref.shape[1] // Q_BLOCK
    per_step = o.shape[0] // Q_BLOCK
    for jj in range(per_step):
        j = pl.program_id(1) * per_step + jj
        o_ref[group, pl.ds(j, Q_BLOCK, stride=n_blocks), :] = o[jj * Q_BLOCK:(jj + 1) * Q_BLOCK, :]


def _attn_out(b, t, groups, tq, blocked):
    shape = jax.ShapeDtypeStruct((b, groups, t, LANES), F32)
    if blocked:
        return pl.BlockSpec((None, groups, t, LANES), lambda bi, qi: (bi, 0, 0, 0)), shape
    return pl.BlockSpec((None, groups, tq, LANES), lambda bi, qi: (bi, 0, qi, 0)), shape


def _diff_attn_kernel(*refs, n_kv, lam_init, blocked):
    q_ref = refs[0]
    kv_refs = refs[1:1 + 2 * n_kv]
    lam_ref, ng_ref, o_ref = refs[1 + 2 * n_kv:]
    lane = lax.broadcasted_iota(jnp.int32, (1, LANES), 1)
    lf = lam_ref[...]
    s1 = jnp.sum(lf[0:1, :] * lf[1:2, :], axis=-1, keepdims=True)
    s2 = jnp.sum(lf[2:3, :] * lf[3:4, :], axis=-1, keepdims=True)
    lmb = jnp.exp(s1) - jnp.exp(s2) + lam_init
    ng = ng_ref[...]
    for h in range(DA_HEADS):
        hs = slice(h * LANES, (h + 1) * LANES)
        q = q_ref[:, hs]
        kvs = [(kv_refs[2 * i][:, hs], _with_ones(kv_refs[2 * i + 1][:, hs])) for i in range(n_kv)]
        outs = []
        for m in range(2):
            in_map = (lane >= m * HD) & (lane < (m + 1) * HD)
            qm = jnp.where(in_map, q, jnp.zeros_like(q))
            outs.append(_softmax_pv(qm, kvs))
        o = outs[0] - lmb * outs[1]
        _store_heads(o_ref, _rms(o, ng) * (1.0 - lam_init), h, blocked)


def _diff_attn(q_src, kv_srcs, lam, ng, lam_init, tq, blocked):
    b, t, _ = q_src.shape
    nq = t // tq
    width = DA_HEADS * LANES
    in_specs = [pl.BlockSpec((None, tq, width), lambda bi, qi: (bi, qi, 0))]
    args = [q_src]
    for src in kv_srcs:
        n = src.shape[1]
        in_specs.append(pl.BlockSpec((None, n, width), lambda bi, qi: (bi, 0, 1)))
        in_specs.append(pl.BlockSpec((None, n, width), lambda bi, qi: (bi, 0, 2)))
        args += [src, src]
    in_specs += [pl.BlockSpec((4, HD), lambda bi, qi: (0, 0)),
                 pl.BlockSpec((1, LANES), lambda bi, qi: (0, 0))]
    args += [lam, ng.reshape(1, LANES)]
    out_spec, out_shape = _attn_out(b, t, DA_HEADS, tq, blocked)
    return pl.pallas_call(
        functools.partial(_diff_attn_kernel, n_kv=len(kv_srcs), lam_init=lam_init, blocked=blocked),
        grid=(b, nq),
        in_specs=in_specs,
        out_specs=out_spec,
        out_shape=out_shape,
        compiler_params=_params("arbitrary", "arbitrary"),
        name="diff_attn",
    )(*args)


def _gqa_kernel(*refs, n_kv, blocked):
    q_ref = refs[0]
    kv_refs = refs[1:1 + 2 * n_kv]
    o_ref = refs[1 + 2 * n_kv]
    lane = lax.broadcasted_iota(jnp.int32, (1, LANES), 1)
    low = lane < HD
    per_kv = GQ_HEADS // GQ_KV
    vals = [_with_ones(kv_refs[2 * i + 1][...]) for i in range(n_kv)]
    for pair in range(GQ_HEADS // 2):
        j = (2 * pair) // per_kv
        kvs = [(kv_refs[2 * i][:, j * LANES:(j + 1) * LANES], vals[i]) for i in range(n_kv)]
        o_a = _softmax_pv(q_ref[:, (2 * pair) * LANES:(2 * pair + 1) * LANES], kvs)
        o_b = _softmax_pv(q_ref[:, (2 * pair + 1) * LANES:(2 * pair + 2) * LANES], kvs)
        if j == 0:
            packed = jnp.where(low, o_a, pltpu.roll(o_b, HD, 1))
        else:
            packed = jnp.where(low, pltpu.roll(o_a, HD, 1), o_b)
        _store_heads(o_ref, packed, pair, blocked)


def _gqa_attn(q_src, kv_srcs, tq, blocked):
    b, t, _ = q_src.shape
    nq = t // tq
    qw = GQ_HEADS * LANES
    in_specs = [pl.BlockSpec((None, tq, qw), lambda bi, qi: (bi, qi, 0))]
    args = [q_src]
    for src in kv_srcs:
        n = src.shape[1]
        in_specs.append(pl.BlockSpec((None, n, GQ_KV * LANES), lambda bi, qi: (bi, 0, qw // (GQ_KV * LANES))))
        in_specs.append(pl.BlockSpec((None, n, LANES), lambda bi, qi: (bi, 0, (qw + GQ_KV * LANES) // LANES)))
        args += [src, src]
    out_spec, out_shape = _attn_out(b, t, GQ_HEADS // 2, tq, blocked)
    return pl.pallas_call(
        functools.partial(_gqa_kernel, n_kv=len(kv_srcs), blocked=blocked),
        grid=(b, nq),
        in_specs=in_specs,
        out_specs=out_spec,
        out_shape=out_shape,
        compiler_params=_params("arbitrary", "arbitrary"),
        name="gqa_attn",
    )(*args)


def _mlstm_kernel(ql_ref, kl_ref, vl_ref, ol_ref, gl_ref, qc_ref, kc_ref, vc_ref, oc_ref, gc_ref,
                  gb_ref, ng_ref, yl_ref, yc_ref, hbl_ref, hbc_ref, cst_ref, mst_ref, *, n_lat, n_ctx):
    c = ML_CHUNK
    lane = lax.broadcasted_iota(jnp.int32, (1, LANES), 1)
    ri = lax.broadcasted_iota(jnp.int32, (c, c), 0)
    ci = lax.broadcasted_iota(jnp.int32, (c, c), 1)
    masks = (ci <= ri, ci >= ri)
    cum_mat = _bf(masks[0].astype(F32))
    row_id = lax.broadcasted_iota(jnp.int32, (c, LANES), 0)
    ones_blk = jnp.ones((c, LANES), F32)
    gb = gb_ref[...]
    k3 = lax.broadcasted_iota(jnp.int32, (2 * LANES, LANES), 0) & (LANES - 1)
    c3 = lax.broadcasted_iota(jnp.int32, (c, 2 * LANES), 1) & (LANES - 1)
    units = [(d, hh) for d in range(2) for hh in range(2)]
    gate_lane = [d * 4 + hh for d, hh in units]
    sel_t = [_bf((k3 == cl).astype(F32)) for cl in gate_lane]
    sel_r = [_bf((c3 == cl).astype(F32)) for cl in gate_lane]
    head_mask = [(lane >= hh * HD) & (lane < (hh + 1) * HD) for hh in range(2)]

    cst_ref[...] = jnp.zeros_like(cst_ref)
    mst_ref[...] = jnp.zeros_like(mst_ref)

    def cummax_rows(x, reverse):
        k = 1
        while k < c:
            if reverse:
                sh = jnp.where(row_id < c - k, pltpu.roll(x, c - k, 0), -jnp.inf)
            else:
                sh = jnp.where(row_id >= k, pltpu.roll(x, k, 0), -jnp.inf)
            x = jnp.maximum(x, sh)
            k *= 2
        return x

    def gate_prep(g_ref, r0, d):
        g = g_ref[pl.ds(r0, c), :] + gb
        lf = _log_sigmoid(g)
        cum = _dot(cum_mat, jnp.concatenate(_split2(lf), axis=1))
        cum = cum[:, :LANES] + cum[:, LANES:]
        if d == 1:
            cum = jnp.sum(lf, axis=0, keepdims=True) - cum + lf
        bc = pltpu.roll(cum, LANES - 2, 1)
        w = g - bc
        cmx = cummax_rows(w, d == 1)
        edge = slice(c - 1, c) if d == 0 else slice(0, 1)
        h, l = _split2(jnp.concatenate([w, cmx, bc], axis=0))
        return jnp.concatenate([h, l], axis=1), bc[edge, :], cmx[edge, :]

    def chunk_step(q_ref, k_ref, v_ref, g_ref, dst_refs, rows):
        preps = [gate_prep(g_ref, rows[d], d) for d in range(2)]
        q2 = [q_ref[pl.ds(rows[d], c), :] for d in range(2)]
        k2 = [k_ref[pl.ds(rows[d], c), :] * (HD ** -0.5) for d in range(2)]
        v2 = [v_ref[pl.ds(rows[d], c), :] for d in range(2)]
        qm = [_bf(jnp.where(head_mask[hh], q2[d], 0.0)) for d, hh in units]
        km = [_bf(jnp.where(head_mask[hh], k2[d], 0.0)) for d, hh in units]
        vh = [v2[d][:, hh * LANES:(hh + 1) * LANES] for d, hh in units]
        bb = [_dot(preps[d][0], sel_t[i]) for i, (d, hh) in enumerate(units)]
        w_b = [x[0:c] for x in bb]
        cm_b = [x[c:2 * c] for x in bb]
        bc_b = [x[2 * c:3 * c] for x in bb]
        w_row = [_dot_nt(sel_r[i], preps[d][0][0:c]) for i, (d, hh) in enumerate(units)]
        qk = _each(_dot_nt, qm, km)
        p = [jnp.exp(jnp.where(masks[d], wr - cm, -jnp.inf)) * s
             for (d, hh), wr, cm, s in zip(units, w_row, cm_b, qk)]
        num_loc = [_dot(_bf(pi), _bf(jnp.concatenate([v, ones_blk], axis=1))) for pi, v in zip(p, vh)]

        m_st = [mst_ref[i][0:1, 0:1] for i in range(4)]
        c_st = [cst_ref[i] for i in range(4)]
        b_end = [preps[d][1][:, cl:cl + 1] for (d, hh), cl in zip(units, gate_lane)]
        w_max = [preps[d][2][:, cl:cl + 1] for (d, hh), cl in zip(units, gate_lane)]
        m_new = [jnp.maximum(be + ms, be + wm) for be, ms, wm in zip(b_end, m_st, w_max)]
        top = _each(jnp.maximum, m_st, cm_b)
        inter = [jnp.exp(ms - t) for ms, t in zip(m_st, top)]
        s_loc = [jnp.exp(cm - t) for cm, t in zip(cm_b, top)]
        floor = [jnp.exp(-(bc + t)) for bc, t in zip(bc_b, top)]
        q_c = [_dot(q, _bf(cs)) for q, cs in zip(qm, c_st)]
        for i, (d, hh) in enumerate(units):
            num = inter[i] * q_c[i][:, :LANES] + s_loc[i] * num_loc[i][:, :LANES]
            den = inter[i] * q_c[i][:, LANES:] + s_loc[i] * num_loc[i][:, LANES:]
            dst_refs[d][pl.ds(rows[d], c), hh * LANES:(hh + 1) * LANES] = num / jnp.maximum(jnp.abs(den), floor[i])
        w_in = [jnp.exp(be + wb - mn) for be, wb, mn in zip(b_end, w_b, m_new)]
        kv = [_dot_tn(k, _bf(jnp.concatenate([w * v, w], axis=1))) for k, w, v in zip(km, w_in, vh)]
        for i in range(4):
            cst_ref[i] = jnp.exp(b_end[i] + m_st[i] - m_new[i]) * c_st[i] + kv[i]
            mst_ref[i] = jnp.broadcast_to(m_new[i], mst_ref.shape[1:])

    for cc in range(n_ctx):
        chunk_step(qc_ref, kc_ref, vc_ref, gc_ref, (yc_ref, hbc_ref), (cc * c, (n_ctx - 1 - cc) * c))

    def body(cc, carry):
        rows = (pl.multiple_of(cc * c, c), pl.multiple_of((n_lat - 1 - cc) * c, c))
        chunk_step(ql_ref, kl_ref, vl_ref, gl_ref, (yl_ref, hbl_ref), rows)
        return carry

    lax.fori_loop(0, n_lat, body, 0, unroll=2)

    ng = ng_ref[...]

    def finish(y_ref, hb_ref, o_ref, n_rows):
        tile = 256
        for r in range(0, n_rows, tile):
            for hh in range(2):
                sl = (slice(r, r + tile), slice(hh * LANES, (hh + 1) * LANES))
                hsum = y_ref[sl] + hb_ref[sl]
                y_ref[sl] = _rms(hsum, ng) * _sigmoid(o_ref[sl])

    finish(yc_ref, hbc_ref, oc_ref, n_ctx * c)
    finish(yl_ref, hbl_ref, ol_ref, n_lat * c)


def _mlstm(pl_f32, pc_f32, gate_bias, ng):
    b, t, _ = pl_f32.shape
    tc = pc_f32.shape[1]
    n_pairs = ML_HEADS // 2
    def specs(n):
        return [pl.BlockSpec((None, n, LANES), lambda bi, p: (bi, 0, p)),
                pl.BlockSpec((None, n, LANES), lambda bi, p: (bi, 0, 2 + p)),
                pl.BlockSpec((None, n, 2 * LANES), lambda bi, p: (bi, 0, 2 + p)),
                pl.BlockSpec((None, n, 2 * LANES), lambda bi, p: (bi, 0, 4 + p)),
                pl.BlockSpec((None, n, LANES), lambda bi, p: (bi, 0, 12 + p))]
    in_specs = specs(t) + specs(tc) + [pl.BlockSpec((None, 1, LANES), lambda bi, p: (p, 0, 0)),
                                      pl.BlockSpec((1, LANES), lambda bi, p: (0, 0))]
    yl, yc = pl.pallas_call(
        functools.partial(_mlstm_kernel, n_lat=t // ML_CHUNK, n_ctx=tc // ML_CHUNK),
        grid=(b, n_pairs),
        in_specs=in_specs,
        out_specs=[pl.BlockSpec((None, t, 2 * LANES), lambda bi, p: (bi, 0, p)),
                   pl.BlockSpec((None, tc, 2 * LANES), lambda bi, p: (bi, 0, p))],
        out_shape=[jax.ShapeDtypeStruct((b, t, ML_HEADS * LANES), F32),
                   jax.ShapeDtypeStruct((b, tc, ML_HEADS * LANES), F32)],
        scratch_shapes=[pltpu.VMEM((t, 2 * LANES), F32), pltpu.VMEM((tc, 2 * LANES), F32),
                        pltpu.VMEM((4, LANES, 2 * LANES), F32), pltpu.VMEM((4, 8, LANES), F32)],
        compiler_params=_params("arbitrary", "arbitrary"),
        name="mlstm",
    )(*([pl_f32] * 5 + [pc_f32] * 5 + [gate_bias, ng.reshape(1, LANES)]))
    return yl, yc


def _each(f, *lists):
    return [f(*xs) for xs in zip(*lists)]


def _unit_tri_inverse(a, eye, ri, ci):
    def mm(u, w):
        return _dot(_bf(u), _bf(w))

    size = a[0].shape[0]
    inside = (ri >> 4) == (ci >> 4)
    d = [jnp.where(inside, ai, 0.0) for ai in a]
    x = [eye - di for di in d]
    p = _each(mm, d, d)
    for step in range(3):
        x = _each(lambda xi, ti: xi + ti, x, _each(mm, x, p))
        if step < 2:
            p = _each(mm, p, p)
    shift = 5
    while (1 << (shift - 1)) < size:
        merged = (ri >> shift) == (ci >> shift)
        e = [jnp.where(merged & jnp.logical_not(inside), ai, 0.0) for ai in a]
        x = _each(lambda xi, ti: xi - ti, x, _each(mm, x, _each(mm, e, x)))
        inside = merged
        shift += 1
    return x


def _dn_kernel(ql_ref, kl_ref, vl_ref, zl_ref, gl_ref, qc_ref, kc_ref, vc_ref, zc_ref, gc_ref,
               wq_ref, wk_ref, wv_ref, alog_ref, dtb_ref, ng_ref, yl_ref, yc_ref,
               sql_ref, skl_ref, svl_ref, sqc_ref, skc_ref, svc_ref, obl_ref, obc_ref,
               p_ref, n_ref, qt_ref, o_ref, eg_ref, *, n_lat, n_ctx):
    c = DN_CHUNK
    lane = lax.broadcasted_iota(jnp.int32, (1, LANES), 1)
    ri = lax.broadcasted_iota(jnp.int32, (c, c), 0)
    ci = lax.broadcasted_iota(jnp.int32, (c, c), 1)
    incl = (ci <= ri, ci >= ri)
    strict = (ci < ri, ci > ri)
    cums = tuple(_bf(m.astype(F32)) for m in incl)
    eye = (ci == ri).astype(F32)
    neg_a = -jnp.exp(alog_ref[...])
    dtb = dtb_ref[...]

    def conv_prep(x_ref, w_ref, dst_ref, kind):
        n = x_ref.shape[0]
        x = x_ref[...]
        rows = lax.broadcasted_iota(jnp.int32, (n, LANES), 0)
        prev = jnp.where(rows == 0, 0.0, pltpu.roll(x, 1, 0))
        nxt = jnp.where(rows == n - 1, 0.0, pltpu.roll(x, n - 1, 0))
        w = w_ref[...]
        y = _silu(prev * w[0:1, :] + x * w[1:2, :] + nxt * w[2:3, :])
        if kind != "v":
            y = y * lax.rsqrt(jnp.sum(y * y, axis=-1, keepdims=True) + EPS)
        if kind == "q":
            y = y * (LANES ** -0.5)
        dst_ref[...] = y

    conv_prep(qc_ref, wq_ref, sqc_ref, "q")
    conv_prep(kc_ref, wk_ref, skc_ref, "k")
    conv_prep(vc_ref, wv_ref, svc_ref, "v")
    conv_prep(ql_ref, wq_ref, sql_ref, "q")
    conv_prep(kl_ref, wk_ref, skl_ref, "k")
    conv_prep(vl_ref, wv_ref, svl_ref, "v")
    lane2 = lax.broadcasted_iota(jnp.int32, (c, 2 * LANES), 1) & (LANES - 1)
    sels = tuple(_bf((lane2 == 2 * d + 1).astype(F32)) for d in range(2))

    def prep_group(q_ref, k_ref, v_ref, g_ref, c0, n_grp, n_seq, base):
        chunks = range(n_grp)
        units = [(u, d) for u in chunks for d in range(2)]
        rows = [pl.multiple_of((c0 + u) * c, c) for u in chunks]
        q = [q_ref[pl.ds(r, c), :] for r in rows]
        k = [k_ref[pl.ds(r, c), :] for r in rows]
        v = [v_ref[pl.ds(r, c), :] for r in rows]
        g = [g_ref[pl.ds(r, c), :] for r in rows]
        qb = [_bf(x) for x in q]
        kb = [_bf(x) for x in k]
        kk = _each(_dot_nt, kb, kb)
        qk = _each(_dot_nt, qb, kb)
        sig = [_sigmoid(x) for x in g]
        gdec = [neg_a * _softplus(x + dtb) for x in g]
        tot = [jnp.sum(x, axis=0, keepdims=True) for x in gdec]
        cum_f = [_dot(cums[0], jnp.concatenate(_split2(x), axis=1)) for x in gdec]
        cum_f = [x[:, :LANES] + x[:, LANES:] for x in cum_f]
        cum = [jnp.where(lane < 2, cf, t - cf + x) for cf, t, x in zip(cum_f, tot, gdec)]
        gcum = [cum[u][:, 2 * d + 1:2 * d + 2] for u, d in units]
        g_end = [tot[u][:, 2 * d + 1:2 * d + 2] for u, d in units]
        beta = [sig[u][:, 2 * d:2 * d + 1] for u, d in units]
        cum_hl = [jnp.concatenate(_split2(x), axis=1) for x in cum]
        g_row = [_dot_nt(sels[d], cum_hl[u]) for u, d in units]
        decay = [jnp.exp(jnp.where(incl[d], gc - gr, -jnp.inf)) for (u, d), gc, gr in zip(units, gcum, g_row)]
        a = [jnp.where(strict[d], bt * dc * kk[u], 0.0) for (u, d), bt, dc in zip(units, beta, decay)]
        tinv = _unit_tri_inverse(a, eye, ri, ci)
        gam = [jnp.exp(x) for x in gcum]
        rhs = [jnp.concatenate([bt * v[u], (bt * gm) * k[u]], axis=1) for (u, d), bt, gm in zip(units, beta, gam)]
        sol = [_bf(_dot(_bf(ti), _bf(r))) for ti, r in zip(tinv, rhs)]
        attn = [_bf(qk[u] * dc) for (u, d), dc in zip(units, decay)]
        kdec = [_bf(k[u] * jnp.exp(ge - gc)) for (u, d), ge, gc in zip(units, g_end, gcum)]
        qo = _each(_dot, attn, sol)
        pn = _each(_dot_tn, kdec, sol)
        for i, (u, d) in enumerate(units):
            chunk = c0 + u
            pos = base + (chunk if d == 0 else n_seq - 1 - chunk)
            p_ref[d, pos] = _bf(pn[i][:, LANES:])
            n_ref[d, pos] = pn[i][:, :LANES]
            qt_ref[d, pos] = _bf(gam[i] * q[u] - qo[i][:, LANES:])
            o_ref[d, pos] = qo[i][:, :LANES]
            eg_ref[d, pos] = jnp.broadcast_to(jnp.exp(g_end[i]), (8, LANES))

    grp_c = math.gcd(n_ctx, DN_GROUP)
    grp_l = math.gcd(n_lat, DN_GROUP)
    for grp in range(n_ctx // grp_c):
        prep_group(sqc_ref, skc_ref, svc_ref, gc_ref, grp * grp_c, grp_c, n_ctx, 0)

    def prep_body(grp, carry):
        prep_group(sql_ref, skl_ref, svl_ref, gl_ref, grp * grp_l, grp_l, n_lat, n_ctx)
        return carry

    lax.fori_loop(0, n_lat // grp_l, prep_body, 0)

    def scan_body(dst_refs, n_seq, base):
        def body(i, states):
            pos = base + i
            rows = (pl.multiple_of(i * c, c), pl.multiple_of((n_seq - 1 - i) * c, c))
            new = []
            for d in range(2):
                s = states[d]
                sb = _bf(s)
                dst_refs[d][pl.ds(rows[d], c), :] = _dot(qt_ref[d, pos], sb) + o_ref[d, pos]
                new.append(eg_ref[d, pos][0:1, :] * s - _dot(p_ref[d, pos], sb) + n_ref[d, pos])
            return tuple(new)
        return body

    zero = jnp.zeros((LANES, LANES), F32)
    states = lax.fori_loop(0, n_ctx, scan_body((yc_ref, obc_ref), n_ctx, 0), (zero, zero))
    lax.fori_loop(0, n_lat, scan_body((yl_ref, obl_ref), n_lat, n_ctx), states, unroll=4)

    ng = ng_ref[...]

    def finish(y_ref, ob_ref, z_ref):
        n_rows = y_ref.shape[0]
        tile = 256
        for r in range(0, n_rows, tile):
            sl = slice(r, r + tile)
            y_ref[sl, :] = _rms(y_ref[sl, :] + ob_ref[sl, :], ng) * _silu(z_ref[sl, :])

    finish(yc_ref, obc_ref, zc_ref)
    finish(yl_ref, obl_ref, zl_ref)


def _deltanet(pl_f32, pc_f32, conv_w, alog_rows, dtb_rows, ng):
    b, t, _ = pl_f32.shape
    tc = pc_f32.shape[1]
    h = DN_HEADS
    n_tot = (t + tc) // DN_CHUNK

    def specs(n):
        return [pl.BlockSpec((None, n, LANES), lambda bi, hi: (bi, 0, hi)),
                pl.BlockSpec((None, n, LANES), lambda bi, hi: (bi, 0, h + hi)),
                pl.BlockSpec((None, n, LANES), lambda bi, hi: (bi, 0, 2 * h + hi)),
                pl.BlockSpec((None, n, LANES), lambda bi, hi: (bi, 0, 3 * h + hi)),
                pl.BlockSpec((None, n, LANES), lambda bi, hi: (bi, 0, 4 * h + hi))]
    in_specs = specs(t) + specs(tc) + [
        pl.BlockSpec((3, LANES), lambda bi, hi: (0, hi)),
        pl.BlockSpec((3, LANES), lambda bi, hi: (0, h + hi)),
        pl.BlockSpec((3, LANES), lambda bi, hi: (0, 2 * h + hi)),
        pl.BlockSpec((None, 1, LANES), lambda bi, hi: (hi, 0, 0)),
        pl.BlockSpec((None, 1, LANES), lambda bi, hi: (hi, 0, 0)),
        pl.BlockSpec((1, LANES), lambda bi, hi: (0, 0))]
    yl, yc = pl.pallas_call(
        functools.partial(_dn_kernel, n_lat=t // DN_CHUNK, n_ctx=tc // DN_CHUNK),
        grid=(b, h),
        in_specs=in_specs,
        out_specs=[pl.BlockSpec((None, t, LANES), lambda bi, hi: (bi, 0, hi)),
                   pl.BlockSpec((None, tc, LANES), lambda bi, hi: (bi, 0, hi))],
        out_shape=[jax.ShapeDtypeStruct((b, t, h * LANES), F32),
                   jax.ShapeDtypeStruct((b, tc, h * LANES), F32)],
        scratch_shapes=[pltpu.VMEM((t, LANES), F32)] * 3 + [pltpu.VMEM((tc, LANES), F32)] * 3
                       + [pltpu.VMEM((t, LANES), F32), pltpu.VMEM((tc, LANES), F32),
                          pltpu.VMEM((2, n_tot, LANES, LANES), BF16), pltpu.VMEM((2, n_tot, LANES, LANES), F32),
                          pltpu.VMEM((2, n_tot, DN_CHUNK, LANES), BF16), pltpu.VMEM((2, n_tot, DN_CHUNK, LANES), F32),
                          pltpu.VMEM((2, n_tot, 8, LANES), F32)],
        compiler_params=_params("arbitrary", "arbitrary"),
        name="deltanet",
    )(*([pl_f32] * 5 + [pc_f32] * 5 + [conv_w, conv_w, conv_w, alog_rows, dtb_rows, ng.reshape(1, LANES)]))
    return yl, yc


_FF_CHUNKS = ((0, 1536), (1536, D_FF))


def _post_kernel(*refs, final):
    if final:
        x_ref, ya_ref, yb_ref, mod_ref, gf_ref, wa_ref, wb_ref, wg_ref, wu_ref, wd_ref, gfin_ref, o_ref = refs
    else:
        x_ref, ya_ref, yb_ref, mod_ref, gf_ref, wa_ref, wb_ref, wg_ref, wu_ref, wd_ref, o_ref = refs
    mod = mod_ref[...]
    ga1 = mod[:, 2 * D_MODEL:3 * D_MODEL]
    ga2 = mod[:, 5 * D_MODEL:6 * D_MODEL]
    ya = jnp.concatenate([_bf(ya_ref[g]) for g in range(ya_ref.shape[0])], axis=1)
    y = _dot(ya, wa_ref[...]) + _dot(_bf(yb_ref[...]), wb_ref[...])
    x1 = x_ref[...] + ga1 * y
    hb = _bf(_norm_mod(x1, gf_ref[...], mod, 3))
    acc = None
    for lo, hi in _FF_CHUNKS:
        gate = _dot(hb, wg_ref[:, lo:hi])
        up = _dot(hb, wu_ref[:, lo:hi])
        part = _dot(_bf(_silu(gate) * up), wd_ref[lo:hi, :])
        acc = part if acc is None else acc + part
    out = x1 + ga2 * acc
    if final:
        out = _rms(out, gfin_ref[...])
    o_ref[...] = out


def _post(x, ya, yb, mods, gf, w_out, oidx, wg, wu, wd, layer, gfin, *, mod_row, tm):
    b, t, _ = x.shape
    rows = b * t
    nb = rows // tm
    final = gfin is not None
    ga_groups = ya.shape[1]
    na, nbw = ga_groups * LANES, yb.shape[-1]
    bpb = t // tm

    def const(shape, slot, row_block=0):
        return pl.BlockSpec((None,) + shape, lambda i: (slot, row_block, 0), pipeline_mode=pl.Buffered(1))
    assert na == nbw
    in_specs = [pl.BlockSpec((tm, D_MODEL), lambda i: (i, 0)),
                pl.BlockSpec((None, ga_groups, tm, LANES), lambda i: (i // bpb, 0, i % bpb, 0)),
                pl.BlockSpec((tm, nbw), lambda i: (i, 0)),
                pl.BlockSpec((None, 1, 6 * D_MODEL), lambda i: (mod_row(i), 0, 0)),
                pl.BlockSpec((1, D_MODEL), lambda i: (0, 0)),
                const((na, D_MODEL), oidx, 0), const((nbw, D_MODEL), oidx, 1),
                const((D_MODEL, D_FF), layer), const((D_MODEL, D_FF), layer), const((D_FF, D_MODEL), layer)]
    args = [x.reshape(rows, D_MODEL), ya, yb.reshape(rows, nbw), mods,
            gf.reshape(1, D_MODEL), w_out, w_out, wg, wu, wd]
    if final:
        in_specs.append(pl.BlockSpec((1, D_MODEL), lambda i: (0, 0)))
        args.append(gfin.reshape(1, D_MODEL))
    out = pl.pallas_call(
        functools.partial(_post_kernel, final=final),
        grid=(nb,),
        in_specs=in_specs,
        out_specs=pl.BlockSpec((tm, D_MODEL), lambda i: (i, 0)),
        out_shape=jax.ShapeDtypeStruct((rows, D_MODEL), F32),
        compiler_params=_params("arbitrary"),
        name="post_ffn",
    )(*args)
    return out.reshape(b, t, D_MODEL)


def _rope_tables(n_tok):
    rows = n_tok // GRID_W
    quarter = HD // 4
    freqs = ROPE_THETA ** (-jnp.arange(quarter, dtype=F32) / quarter)
    row = jnp.repeat(jnp.arange(rows, dtype=F32), GRID_W)
    col = jnp.tile(jnp.arange(GRID_W, dtype=F32), rows)
    ang = jnp.concatenate([row[:, None] * freqs, col[:, None] * freqs], axis=-1)
    cos, sin = jnp.cos(ang), jnp.sin(ang)
    lane = np.arange(LANES)
    d = lane % HD
    src = (d // 32) * quarter + (d % quarter)
    first = (d % 32) < quarter
    c = cos[:, src]
    s = sin[:, src]
    s1 = jnp.where(first[None, :], -s, 0.0)
    s2 = jnp.where(first[None, :], 0.0, s)
    return c, s1, s2


def _place(cols_src, width, dst_lanes):
    idx = np.zeros((width,), np.int32)
    msk = np.zeros((width,), bool)
    for src, dst in zip(cols_src, dst_lanes):
        idx[dst] = src
        msk[dst] = True
    return idx, msk


def _gather_cols(w, idx, msk):
    parts = []
    n = len(idx)
    i = 0
    while i < n:
        j = i
        if not msk[i]:
            while j < n and not msk[j]:
                j += 1
            parts.append(jnp.zeros(w.shape[:-1] + (j - i,), w.dtype))
        else:
            while j + 1 < n and msk[j + 1] and idx[j + 1] == idx[j] + 1:
                j += 1
            j += 1
            parts.append(w[..., int(idx[i]):int(idx[i]) + (j - i)])
        i = j
    return jnp.concatenate(parts, axis=-1)


def _even_layout():
    src, dst = [], []
    for i in range(3072):
        src.append(i)
        dst.append(i)
    gbase = 3072
    for p in range(2):
        for kind in range(4):
            for hh in range(2):
                src.append(gbase + kind * ML_HEADS + 2 * p + hh)
                dst.append(3072 + p * LANES + kind * 2 + hh)
    return _place(src, EVEN_BF + EVEN_F32, dst)


def _odd_layout():
    src, dst = [], []
    per_kv = GQ_HEADS // GQ_KV
    for h in range(GQ_HEADS):
        j = h // per_kv
        for e in range(HD):
            src.append(h * HD + e)
            dst.append(h * LANES + j * HD + e)
    for j in range(GQ_KV):
        for e in range(HD):
            src.append(512 + j * HD + e)
            dst.append(GQ_HEADS * LANES + j * LANES + j * HD + e)
    for e in range(GQ_KV * HD):
        src.append(640 + e)
        dst.append((GQ_HEADS + GQ_KV) * LANES + e)
    for e in range(1536 + 512):
        src.append(768 + e)
        dst.append(ODD_BF + e)
    for h in range(DN_HEADS):
        for d in range(2):
            for kind in range(2):
                src.append(2816 + d * 2 * DN_HEADS + kind * DN_HEADS + h)
                dst.append(ODD_BF + 2048 + h * LANES + d * 2 + kind)
    return _place(src, ODD_BF + ODD_F32, dst)


def kernel(x, c, ctx, c_ctx, w_ada, b_ada, g_mix, g_ffn, w_in_e, b_gate_e, da_lam, da_norm_g, ml_norm_g,
           w_out_e, w_in_o, qk_norm_g, dn_conv, dn_a_log, dn_dt_bias, dn_norm_g, w_out_o, w_gate, w_up,
           w_down, g_final):
    b, t, _ = x.shape
    tc = ctx.shape[1]
    tables = _rope_tables(t)

    cv = jnp.zeros((16, D_MODEL), F32).at[:b].set(c).at[b].set(c_ctx)
    mods = _ada_mods(cv, w_ada, b_ada).reshape(DEPTH, 16, 1, 6 * D_MODEL)

    w_e = _gather_cols(_bf(w_in_e), *_even_layout())
    w_o = _gather_cols(_bf(w_in_o), *_odd_layout())
    wout_e, wout_o = _bf(w_out_e), _bf(w_out_o)
    wg, wu, wd = _bf(w_gate), _bf(w_up), _bf(w_down)
    tm_l = 512
    tm_c = 256
    bpb = t // tm_l
    lat_row = lambda i: i // bpb
    ctx_row = lambda i: b

    xl, xc = x, ctx
    for layer in range(DEPTH):
        emit_ctx = layer < DEPTH - 1
        m = mods[layer]
        if layer % 2 == 0:
            e = layer // 2
            lam_init = 0.8 - 0.6 * math.exp(-0.3 * layer)
            pl_bf, pl_f = _inproj(xl, m, g_mix[layer], w_e, e, [], tables, even=True, seq_len=t, mod_row=lat_row, tm=tm_l)
            pc_bf, pc_f = _inproj(xc, m, g_mix[layer], w_e, e, [], None, even=True, seq_len=tc, mod_row=ctx_row, tm=tm_c)
            ya_l = _diff_attn(pl_bf, [pc_bf, pl_bf], da_lam[e], da_norm_g[e], lam_init, 512, True)
            gbias = jnp.zeros((2, 1, LANES), F32)
            gb = b_gate_e[e].reshape(4, 2, 2)
            gbias = gbias.at[:, 0, :8].set(gb.transpose(1, 0, 2).reshape(2, 8))
            yb_l, yb_c = _mlstm(pl_f, pc_f, gbias, ml_norm_g[e])
            if emit_ctx:
                ya_c = _diff_attn(pc_bf, [pc_bf], da_lam[e], da_norm_g[e], lam_init, tc, False)
            w_out, oidx = wout_e, e
        else:
            o = layer // 2
            qkg = jnp.zeros((2 * GQ_KV, LANES), F32)
            for j in range(GQ_KV):
                qkg = qkg.at[j, j * HD:(j + 1) * HD].set(qk_norm_g[o, 0])
                qkg = qkg.at[GQ_KV + j, j * HD:(j + 1) * HD].set(qk_norm_g[o, 1])
            pl_bf, pl_f = _inproj(xl, m, g_mix[layer], w_o, o, [qkg], tables, even=False, seq_len=t, mod_row=lat_row, tm=tm_l)
            pc_bf, pc_f = _inproj(xc, m, g_mix[layer], w_o, o, [qkg], None, even=False, seq_len=tc, mod_row=ctx_row, tm=tm_c)
            ya_l = _gqa_attn(pl_bf, [pc_bf, pl_bf], 512, True)
            alog = jnp.zeros((DN_HEADS, 1, LANES), F32)
            dtb = jnp.zeros((DN_HEADS, 1, LANES), F32)
            for d in range(2):
                alog = alog.at[:, 0, 2 * d + 1].set(dn_a_log[o, d])
                dtb = dtb.at[:, 0, 2 * d + 1].set(dn_dt_bias[o, d])
            yb_l, yb_c = _deltanet(pl_f, pc_f, dn_conv[o], alog, dtb, dn_norm_g[o])
            if emit_ctx:
                ya_c = _gqa_attn(pc_bf, [pc_bf], tc, False)
            w_out, oidx = wout_o, o
        gfin = g_final if layer == DEPTH - 1 else None
        xl = _post(xl, ya_l, yb_l, m, g_ffn[layer], w_out, oidx, wg, wu, wd, layer, gfin, mod_row=lat_row, tm=tm_l)
        if emit_ctx:
            xc = _post(xc, ya_c, yb_c, m, g_ffn[layer], w_out, oidx, wg, wu, wd, layer, None, mod_row=ctx_row, tm=tm_c)
    return xl
```

```python
import functools
import math

import jax
import jax.numpy as jnp
import numpy as np
from jax import lax
from jax.experimental import pallas as pl
from jax.experimental.pallas import tpu as pltpu

F32 = jnp.float32
BF16 = jnp.bfloat16

D_MODEL = 1024
DEPTH = 4
GRID_W = 64
ROPE_THETA = 10000.0
EPS = 1e-6
HD = 64
Q_BLOCK = 128
LANES = 128
DA_HEADS = 4
ML_HEADS = 4
GQ_HEADS = 8
GQ_KV = 2
DN_HEADS = 4
D_FF = 2816
ML_CHUNK = 128
DN_CHUNK = 128
DN_GROUP = 8
VMEM_LIMIT = 56 * 1024 * 1024

EVEN_BF = 1536
EVEN_F32 = 1792
ODD_BF = 1408
ODD_F32 = 2560


def _bf(x):
    return x.astype(BF16)


def _dot(a, b):
    return jnp.dot(a, b, preferred_element_type=F32)


def _dot_nt(a, b):
    return lax.dot_general(a, b, (((1,), (1,)), ((), ())), preferred_element_type=F32)


def _dot_tn(a, b):
    return lax.dot_general(a, b, (((0,), (0,)), ((), ())), preferred_element_type=F32)


def _split2(x):
    hi = _bf(x)
    return hi, _bf(x - hi.astype(F32))


def _dot_x3(a, b):
    ah, al = _split2(a)
    bh, bl = _split2(b)
    return _dot(ah, bh) + (_dot(ah, bl) + _dot(al, bh))


def _sigmoid(x):
    return 1.0 / (1.0 + jnp.exp(-x))


def _silu(x):
    return x * _sigmoid(x)


def _log_sigmoid(x):
    return jnp.minimum(x, 0.0) - jnp.log(1.0 + jnp.exp(-jnp.abs(x)))


def _softplus(x):
    return jnp.maximum(x, 0.0) + jnp.log(1.0 + jnp.exp(-jnp.abs(x)))


def _rms(x, g, n=None):
    n = x.shape[-1] if n is None else n
    ss = jnp.sum(x * x, axis=-1, keepdims=True)
    return x * lax.rsqrt(ss * (1.0 / n) + EPS) * g


def _norm_mod(x, g, mod, slot):
    sh = mod[:, slot * D_MODEL:(slot + 1) * D_MODEL]
    sc = mod[:, (slot + 1) * D_MODEL:(slot + 2) * D_MODEL]
    return _rms(x, g) * (1.0 + sc) + sh


def _rope(x, c, s1, s2):
    return x * c + pltpu.roll(x, LANES - 16, 1) * s1 + pltpu.roll(x, 16, 1) * s2


def _params(*sem):
    return pltpu.CompilerParams(dimension_semantics=sem, vmem_limit_bytes=VMEM_LIMIT)


def _ada_kernel(cv_ref, w_ref, b_ref, o_ref):
    cv = cv_ref[...]
    o_ref[...] = _dot_x3(_silu(cv), w_ref[...]) + b_ref[...]


def _ada_mods(cv, w_ada, b_ada):
    rows = cv.shape[0]
    nb = 6
    return pl.pallas_call(
        _ada_kernel,
        grid=(DEPTH, nb),
        in_specs=[pl.BlockSpec((rows, D_MODEL), lambda l, j: (0, 0)),
                  pl.BlockSpec((None, D_MODEL, D_MODEL), lambda l, j: (l, 0, j)),
                  pl.BlockSpec((None, 1, D_MODEL), lambda l, j: (l, 0, j))],
        out_specs=pl.BlockSpec((None, rows, D_MODEL), lambda l, j: (l, 0, j)),
        out_shape=jax.ShapeDtypeStruct((DEPTH, rows, 6 * D_MODEL), F32),
        compiler_params=_params("arbitrary", "arbitrary"),
        name="ada_mods",
    )(cv, w_ada, b_ada.reshape(DEPTH, 1, 6 * D_MODEL))


def _inproj_even_kernel(*refs, rope):
    if rope:
        x_ref, mod_ref, g_ref, w_ref, rc_ref, s1_ref, s2_ref, o1_ref, o2_ref = refs
    else:
        x_ref, mod_ref, g_ref, w_ref, o1_ref, o2_ref = refs
    hb = _bf(_norm_mod(x_ref[...], g_ref[...], mod_ref[...], 0))
    for seg in range(4):
        p = _dot(hb, w_ref[:, seg * 256:(seg + 1) * 256])
        for half in range(2):
            gi = seg * 2 + half
            xg = p[:, half * LANES:(half + 1) * LANES]
            if rope:
                xg = _rope(xg, rc_ref[...], s1_ref[...], s2_ref[...])
            if gi < DA_HEADS:
                xg = xg * (HD ** -0.5)
            o1_ref[:, gi * LANES:(gi + 1) * LANES] = _bf(xg)
    o1_ref[:, 1024:EVEN_BF] = _bf(_dot(hb, w_ref[:, 1024:EVEN_BF]))
    o2_ref[...] = _dot(hb, w_ref[:, EVEN_BF:])


def _inproj_odd_kernel(*refs, rope):
    if rope:
        x_ref, mod_ref, g_ref, w_ref, qkg_ref, rc_ref, s1_ref, s2_ref, o1_ref, o2_ref = refs
    else:
        x_ref, mod_ref, g_ref, w_ref, qkg_ref, o1_ref, o2_ref = refs
    hb = _bf(_norm_mod(x_ref[...], g_ref[...], mod_ref[...], 0))
    for seg in range(5):
        p = _dot(hb, w_ref[:, seg * 256:(seg + 1) * 256])
        for half in range(2):
            gi = seg * 2 + half
            xg = p[:, half * LANES:(half + 1) * LANES]
            if gi < GQ_HEADS:
                grow = gi // (GQ_HEADS // GQ_KV)
            else:
                grow = GQ_KV + (gi - GQ_HEADS)
            xg = _rms(xg, qkg_ref[grow:grow + 1, :], HD)
            if rope:
                xg = _rope(xg, rc_ref[...], s1_ref[...], s2_ref[...])
            if gi < GQ_HEADS:
                xg = xg * (HD ** -0.5)
            o1_ref[:, gi * LANES:(gi + 1) * LANES] = _bf(xg)
    o1_ref[:, 1280:ODD_BF] = _bf(_dot(hb, w_ref[:, 1280:ODD_BF]))
    o2_ref[...] = _dot(hb, w_ref[:, ODD_BF:])


def _inproj(x, mods, g, w, widx, extra, tables, *, even, seq_len, mod_row, tm):
    b, t, _ = x.shape
    rows = b * t
    nb = rows // tm
    bpb = max(t // tm, 1)
    n_bf, n_f32 = (EVEN_BF, EVEN_F32) if even else (ODD_BF, ODD_F32)
    rope = tables is not None
    kern = functools.partial(_inproj_even_kernel if even else _inproj_odd_kernel, rope=rope)
    in_specs = [pl.BlockSpec((tm, D_MODEL), lambda i: (i, 0)),
                pl.BlockSpec((None, 1, 6 * D_MODEL), lambda i: (mod_row(i), 0, 0)),
                pl.BlockSpec((1, D_MODEL), lambda i: (0, 0)),
                pl.BlockSpec((None, D_MODEL, n_bf + n_f32), lambda i: (widx, 0, 0))]
    args = [x.reshape(rows, D_MODEL), mods, g.reshape(1, D_MODEL), w]
    for e in extra:
        in_specs.append(pl.BlockSpec(e.shape, lambda i: (0, 0)))
        args.append(e)
    if rope:
        for tab in tables:
            in_specs.append(pl.BlockSpec((tm, LANES), lambda i: (i % bpb, 0)))
            args.append(tab)
    o1, o2 = pl.pallas_call(
        kern,
        grid=(nb,),
        in_specs=in_specs,
        out_specs=[pl.BlockSpec((tm, n_bf), lambda i: (i, 0)),
                   pl.BlockSpec((tm, n_f32), lambda i: (i, 0))],
        out_shape=[jax.ShapeDtypeStruct((rows, n_bf), BF16),
                   jax.ShapeDtypeStruct((rows, n_f32), F32)],
        compiler_params=_params("arbitrary"),
        name="inproj_even" if even else "inproj_odd",
    )(*args)
    return o1.reshape(b, t, n_bf), o2.reshape(b, t, n_f32)


def _with_ones(v):
    return jnp.concatenate([v, jnp.ones_like(v)], axis=1)


def _softmax_pv(q, kvs):
    ss = [_dot_nt(q, k) for k, _ in kvs]
    mx = ss[0].max(axis=-1, keepdims=True)
    for s in ss[1:]:
        mx = jnp.maximum(mx, s.max(axis=-1, keepdims=True))
    acc = None
    for s, (_, v) in zip(ss, kvs):
        o = _dot(_bf(jnp.exp(s - mx)), v)
        acc = o if acc is None else acc + o
    return acc[:, :LANES] / acc[:, LANES:]


def _store_heads(o_ref, o, group, blocked):
    if not blocked:
        o_ref[group] = o
        return
    n_blocks = o_ref.shape[1] // Q_BLOCK
    per_step = o.shape[0] // Q_BLOCK
    for jj in range(per_step):
        j = pl.program_id(1) * per_step + jj
        o_ref[group, pl.ds(j, Q_BLOCK, stride=n_blocks), :] = o[jj * Q_BLOCK:(jj + 1) * Q_BLOCK, :]


def _attn_out(b, t, groups, tq, blocked):
    shape = jax.ShapeDtypeStruct((b, groups, t, LANES), F32)
    if blocked:
        return pl.BlockSpec((None, groups, t, LANES), lambda bi, qi: (bi, 0, 0, 0)), shape
    return pl.BlockSpec((None, groups, tq, LANES), lambda bi, qi: (bi, 0, qi, 0)), shape


def _diff_attn_kernel(*refs, n_kv, lam_init, blocked):
    q_ref = refs[0]
    kv_refs = refs[1:1 + 2 * n_kv]
    lam_ref, ng_ref, o_ref = refs[1 + 2 * n_kv:]
    lane = lax.broadcasted_iota(jnp.int32, (1, LANES), 1)
    lf = lam_ref[...]
    s1 = jnp.sum(lf[0:1, :] * lf[1:2, :], axis=-1, keepdims=True)
    s2 = jnp.sum(lf[2:3, :] * lf[3:4, :], axis=-1, keepdims=True)
    lmb = jnp.exp(s1) - jnp.exp(s2) + lam_init
    ng = ng_ref[...]
    for h in range(DA_HEADS):
        hs = slice(h * LANES, (h + 1) * LANES)
        q = q_ref[:, hs]
        kvs = [(kv_refs[2 * i][:, hs], _with_ones(kv_refs[2 * i + 1][:, hs])) for i in range(n_kv)]
        outs = []
        for m in range(2):
            in_map = (lane >= m * HD) & (lane < (m + 1) * HD)
            qm = jnp.where(in_map, q, jnp.zeros_like(q))
            outs.append(_softmax_pv(qm, kvs))
        o = outs[0] - lmb * outs[1]
        _store_heads(o_ref, _rms(o, ng) * (1.0 - lam_init), h, blocked)


def _diff_attn(q_src, kv_srcs, lam, ng, lam_init, tq, blocked):
    b, t, _ = q_src.shape
    nq = t // tq
    width = DA_HEADS * LANES
    in_specs = [pl.BlockSpec((None, tq, width), lambda bi, qi: (bi, qi, 0))]
    args = [q_src]
    for src in kv_srcs:
        n = src.shape[1]
        in_specs.append(pl.BlockSpec((None, n, width), lambda bi, qi: (bi, 0, 1)))
        in_specs.append(pl.BlockSpec((None, n, width), lambda bi, qi: (bi, 0, 2)))
        args += [src, src]
    in_specs += [pl.BlockSpec((4, HD), lambda bi, qi: (0, 0)),
                 pl.BlockSpec((1, LANES), lambda bi, qi: (0, 0))]
    args += [lam, ng.reshape(1, LANES)]
    out_spec, out_shape = _attn_out(b, t, DA_HEADS, tq, blocked)
    return pl.pallas_call(
        functools.partial(_diff_attn_kernel, n_kv=len(kv_srcs), lam_init=lam_init, blocked=blocked),
        grid=(b, nq),
        in_specs=in_specs,
        out_specs=out_spec,
        out_shape=out_shape,
        compiler_params=_params("arbitrary", "arbitrary"),
        name="diff_attn",
    )(*args)


def _gqa_kernel(*refs, n_kv, blocked):
    q_ref = refs[0]
    kv_refs = refs[1:1 + 2 * n_kv]
    o_ref = refs[1 + 2 * n_kv]
    lane = lax.broadcasted_iota(jnp.int32, (1, LANES), 1)
    low = lane < HD
    per_kv = GQ_HEADS // GQ_KV
    vals = [_with_ones(kv_refs[2 * i + 1][...]) for i in range(n_kv)]
    for pair in range(GQ_HEADS // 2):
        j = (2 * pair) // per_kv
        kvs = [(kv_refs[2 * i][:, j * LANES:(j + 1) * LANES], vals[i]) for i in range(n_kv)]
        o_a = _softmax_pv(q_ref[:, (2 * pair) * LANES:(2 * pair + 1) * LANES], kvs)
        o_b = _softmax_pv(q_ref[:, (2 * pair + 1) * LANES:(2 * pair + 2) * LANES], kvs)
        if j == 0:
            packed = jnp.where(low, o_a, pltpu.roll(o_b, HD, 1))
        else:
            packed = jnp.where(low, pltpu.roll(o_a, HD, 1), o_b)
        _store_heads(o_ref, packed, pair, blocked)


def _gqa_attn(q_src, kv_srcs, tq, blocked):
    b, t, _ = q_src.shape
    nq = t // tq
    qw = GQ_HEADS * LANES
    in_specs = [pl.BlockSpec((None, tq, qw), lambda bi, qi: (bi, qi, 0))]
    args = [q_src]
    for src in kv_srcs:
        n = src.shape[1]
        in_specs.append(pl.BlockSpec((None, n, GQ_KV * LANES), lambda bi, qi: (bi, 0, qw // (GQ_KV * LANES))))
        in_specs.append(pl.BlockSpec((None, n, LANES), lambda bi, qi: (bi, 0, (qw + GQ_KV * LANES) // LANES)))
        args += [src, src]
    out_spec, out_shape = _attn_out(b, t, GQ_HEADS // 2, tq, blocked)
    return pl.pallas_call(
        functools.partial(_gqa_kernel, n_kv=len(kv_srcs), blocked=blocked),
        grid=(b, nq),
        in_specs=in_specs,
        out_specs=out_spec,
        out_shape=out_shape,
        compiler_params=_params("arbitrary", "arbitrary"),
        name="gqa_attn",
    )(*args)


def _mlstm_kernel(ql_ref, kl_ref, vl_ref, ol_ref, gl_ref, qc_ref, kc_ref, vc_ref, oc_ref, gc_ref,
                  gb_ref, ng_ref, yl_ref, yc_ref, hbl_ref, hbc_ref, cst_ref, mst_ref, *, n_lat, n_ctx):
    c = ML_CHUNK
    lane = lax.broadcasted_iota(jnp.int32, (1, LANES), 1)
    ri = lax.broadcasted_iota(jnp.int32, (c, c), 0)
    ci = lax.broadcasted_iota(jnp.int32, (c, c), 1)
    masks = (ci <= ri, ci >= ri)
    cum_mat = _bf(masks[0].astype(F32))
    row_id = lax.broadcasted_iota(jnp.int32, (c, LANES), 0)
    ones_blk = jnp.ones((c, LANES), F32)
    gb = gb_ref[...]
    k3 = lax.broadcasted_iota(jnp.int32, (2 * LANES, LANES), 0) & (LANES - 1)
    c3 = lax.broadcasted_iota(jnp.int32, (c, 2 * LANES), 1) & (LANES - 1)
    units = [(d, hh) for d in range(2) for hh in range(2)]
    gate_lane = [d * 4 + hh for d, hh in units]
    sel_t = [_bf((k3 == cl).astype(F32)) for cl in gate_lane]
    sel_r = [_bf((c3 == cl).astype(F32)) for cl in gate_lane]
    head_mask = [(lane >= hh * HD) & (lane < (hh + 1) * HD) for hh in range(2)]

    cst_ref[...] = jnp.zeros_like(cst_ref)
    mst_ref[...] = jnp.zeros_like(mst_ref)

    def cummax_rows(x, reverse):
        k = 1
        while k < c:
            if reverse:
                sh = jnp.where(row_id < c - k, pltpu.roll(x, c - k, 0), -jnp.inf)
            else:
                sh = jnp.where(row_id >= k, pltpu.roll(x, k, 0), -jnp.inf)
            x = jnp.maximum(x, sh)
            k *= 2
        return x

    def gate_prep(g_ref, r0, d):
        g = g_ref[pl.ds(r0, c), :] + gb
        lf = _log_sigmoid(g)
        cum = _dot(cum_mat, jnp.concatenate(_split2(lf), axis=1))
        cum = cum[:, :LANES] + cum[:, LANES:]
        if d == 1:
            cum = jnp.sum(lf, axis=0, keepdims=True) - cum + lf
        bc = pltpu.roll(cum, LANES - 2, 1)
        w = g - bc
        cmx = cummax_rows(w, d == 1)
        edge = slice(c - 1, c) if d == 0 else slice(0, 1)
        h, l = _split2(jnp.concatenate([w, cmx, bc], axis=0))
        return jnp.concatenate([h, l], axis=1), bc[edge, :], cmx[edge, :]

    def chunk_step(q_ref, k_ref, v_ref, g_ref, dst_refs, rows):
        preps = [gate_prep(g_ref, rows[d], d) for d in range(2)]
        q2 = [q_ref[pl.ds(rows[d], c), :] for d in range(2)]
        k2 = [k_ref[pl.ds(rows[d], c), :] * (HD ** -0.5) for d in range(2)]
        v2 = [v_ref[pl.ds(rows[d], c), :] for d in range(2)]
        qm = [_bf(jnp.where(head_mask[hh], q2[d], 0.0)) for d, hh in units]
        km = [_bf(jnp.where(head_mask[hh], k2[d], 0.0)) for d, hh in units]
        vh = [v2[d][:, hh * LANES:(hh + 1) * LANES] for d, hh in units]
        bb = [_dot(preps[d][0], sel_t[i]) for i, (d, hh) in enumerate(units)]
        w_b = [x[0:c] for x in bb]
        cm_b = [x[c:2 * c] for x in bb]
        bc_b = [x[2 * c:3 * c] for x in bb]
        w_row = [_dot_nt(sel_r[i], preps[d][0][0:c]) for i, (d, hh) in enumerate(units)]
        qk = _each(_dot_nt, qm, km)
        p = [jnp.exp(jnp.where(masks[d], wr - cm, -jnp.inf)) * s
             for (d, hh), wr, cm, s in zip(units, w_row, cm_b, qk)]
        num_loc = [_dot(_bf(pi), _bf(jnp.concatenate([v, ones_blk], axis=1))) for pi, v in zip(p, vh)]

        m_st = [mst_ref[i][0:1, 0:1] for i in range(4)]
        c_st = [cst_ref[i] for i in range(4)]
        b_end = [preps[d][1][:, cl:cl + 1] for (d, hh), cl in zip(units, gate_lane)]
        w_max = [preps[d][2][:, cl:cl + 1] for (d, hh), cl in zip(units, gate_lane)]
        m_new = [jnp.maximum(be + ms, be + wm) for be, ms, wm in zip(b_end, m_st, w_max)]
        top = _each(jnp.maximum, m_st, cm_b)
        inter = [jnp.exp(ms - t) for ms, t in zip(m_st, top)]
        s_loc = [jnp.exp(cm - t) for cm, t in zip(cm_b, top)]
        floor = [jnp.exp(-(bc + t)) for bc, t in zip(bc_b, top)]
        q_c = [_dot(q, _bf(cs)) for q, cs in zip(qm, c_st)]
        for i, (d, hh) in enumerate(units):
            num = inter[i] * q_c[i][:, :LANES] + s_loc[i] * num_loc[i][:, :LANES]
            den = inter[i] * q_c[i][:, LANES:] + s_loc[i] * num_loc[i][:, LANES:]
            dst_refs[d][pl.ds(rows[d], c), hh * LANES:(hh + 1) * LANES] = num / jnp.maximum(jnp.abs(den), floor[i])
        w_in = [jnp.exp(be + wb - mn) for be, wb, mn in zip(b_end, w_b, m_new)]
        kv = [_dot_tn(k, _bf(jnp.concatenate([w * v, w], axis=1))) for k, w, v in zip(km, w_in, vh)]
        for i in range(4):
            cst_ref[i] = jnp.exp(b_end[i] + m_st[i] - m_new[i]) * c_st[i] + kv[i]
            mst_ref[i] = jnp.broadcast_to(m_new[i], mst_ref.shape[1:])

    for cc in range(n_ctx):
        chunk_step(qc_ref, kc_ref, vc_ref, gc_ref, (yc_ref, hbc_ref), (cc * c, (n_ctx - 1 - cc) * c))

    def body(cc, carry):
        rows = (pl.multiple_of(cc * c, c), pl.multiple_of((n_lat - 1 - cc) * c, c))
        chunk_step(ql_ref, kl_ref, vl_ref, gl_ref, (yl_ref, hbl_ref), rows)
        return carry

    lax.fori_loop(0, n_lat, body, 0, unroll=2)

    ng = ng_ref[...]

    def finish(y_ref, hb_ref, o_ref, n_rows):
        tile = 256
        for r in range(0, n_rows, tile):
            for hh in range(2):
                sl = (slice(r, r + tile), slice(hh * LANES, (hh + 1) * LANES))
                hsum = y_ref[sl] + hb_ref[sl]
                y_ref[sl] = _rms(hsum, ng) * _sigmoid(o_ref[sl])

    finish(yc_ref, hbc_ref, oc_ref, n_ctx * c)
    finish(yl_ref, hbl_ref, ol_ref, n_lat * c)


def _mlstm(pl_f32, pc_f32, gate_bias, ng):
    b, t, _ = pl_f32.shape
    tc = pc_f32.shape[1]
    n_pairs = ML_HEADS // 2
    def specs(n):
        return [pl.BlockSpec((None, n, LANES), lambda bi, p: (bi, 0, p)),
                pl.BlockSpec((None, n, LANES), lambda bi, p: (bi, 0, 2 + p)),
                pl.BlockSpec((None, n, 2 * LANES), lambda bi, p: (bi, 0, 2 + p)),
                pl.BlockSpec((None, n, 2 * LANES), lambda bi, p: (bi, 0, 4 + p)),
                pl.BlockSpec((None, n, LANES), lambda bi, p: (bi, 0, 12 + p))]
    in_specs = specs(t) + specs(tc) + [pl.BlockSpec((None, 1, LANES), lambda bi, p: (p, 0, 0)),
                                      pl.BlockSpec((1, LANES), lambda bi, p: (0, 0))]
    yl, yc = pl.pallas_call(
        functools.partial(_mlstm_kernel, n_lat=t // ML_CHUNK, n_ctx=tc // ML_CHUNK),
        grid=(b, n_pairs),
        in_specs=in_specs,
        out_specs=[pl.BlockSpec((None, t, 2 * LANES), lambda bi, p: (bi, 0, p)),
                   pl.BlockSpec((None, tc, 2 * LANES), lambda bi, p: (bi, 0, p))],
        out_shape=[jax.ShapeDtypeStruct((b, t, ML_HEADS * LANES), F32),
                   jax.ShapeDtypeStruct((b, tc, ML_HEADS * LANES), F32)],
        scratch_shapes=[pltpu.VMEM((t, 2 * LANES), F32), pltpu.VMEM((tc, 2 * LANES), F32),
                        pltpu.VMEM((4, LANES, 2 * LANES), F32), pltpu.VMEM((4, 8, LANES), F32)],
        compiler_params=_params("arbitrary", "arbitrary"),
        name="mlstm",
    )(*([pl_f32] * 5 + [pc_f32] * 5 + [gate_bias, ng.reshape(1, LANES)]))
    return yl, yc


def _each(f, *lists):
    return [f(*xs) for xs in zip(*lists)]


def _unit_tri_inverse(a, eye, ri, ci):
    def mm(u, w):
        return _dot(_bf(u), _bf(w))

    size = a[0].shape[0]
    inside = (ri >> 4) == (ci >> 4)
    d = [jnp.where(inside, ai, 0.0) for ai in a]
    x = [eye - di for di in d]
    p = _each(mm, d, d)
    for step in range(3):
        x = _each(lambda xi, ti: xi + ti, x, _each(mm, x, p))
        if step < 2:
            p = _each(mm, p, p)
    shift = 5
    while (1 << (shift - 1)) < size:
        merged = (ri >> shift) == (ci >> shift)
        e = [jnp.where(merged & jnp.logical_not(inside), ai, 0.0) for ai in a]
        x = _each(lambda xi, ti: xi - ti, x, _each(mm, x, _each(mm, e, x)))
        inside = merged
        shift += 1
    return x


def _dn_kernel(ql_ref, kl_ref, vl_ref, zl_ref, gl_ref, qc_ref, kc_ref, vc_ref, zc_ref, gc_ref,
               wq_ref, wk_ref, wv_ref, alog_ref, dtb_ref, ng_ref, yl_ref, yc_ref,
               sql_ref, skl_ref, svl_ref, sqc_ref, skc_ref, svc_ref, obl_ref, obc_ref,
               p_ref, n_ref, qt_ref, o_ref, eg_ref, *, n_lat, n_ctx):
    c = DN_CHUNK
    lane = lax.broadcasted_iota(jnp.int32, (1, LANES), 1)
    ri = lax.broadcasted_iota(jnp.int32, (c, c), 0)
    ci = lax.broadcasted_iota(jnp.int32, (c, c), 1)
    incl = (ci <= ri, ci >= ri)
    strict = (ci < ri, ci > ri)
    cums = tuple(_bf(m.astype(F32)) for m in incl)
    eye = (ci == ri).astype(F32)
    neg_a = -jnp.exp(alog_ref[...])
    dtb = dtb_ref[...]

    def conv_prep(x_ref, w_ref, dst_ref, kind):
        n = x_ref.shape[0]
        x = x_ref[...]
        rows = lax.broadcasted_iota(jnp.int32, (n, LANES), 0)
        prev = jnp.where(rows == 0, 0.0, pltpu.roll(x, 1, 0))
        nxt = jnp.where(rows == n - 1, 0.0, pltpu.roll(x, n - 1, 0))
        w = w_ref[...]
        y = _silu(prev * w[0:1, :] + x * w[1:2, :] + nxt * w[2:3, :])
        if kind != "v":
            y = y * lax.rsqrt(jnp.sum(y * y, axis=-1, keepdims=True) + EPS)
        if kind == "q":
            y = y * (LANES ** -0.5)
        dst_ref[...] = y

    conv_prep(qc_ref, wq_ref, sqc_ref, "q")
    conv_prep(kc_ref, wk_ref, skc_ref, "k")
    conv_prep(vc_ref, wv_ref, svc_ref, "v")
    conv_prep(ql_ref, wq_ref, sql_ref, "q")
    conv_prep(kl_ref, wk_ref, skl_ref, "k")
    conv_prep(vl_ref, wv_ref, svl_ref, "v")
    lane2 = lax.broadcasted_iota(jnp.int32, (c, 2 * LANES), 1) & (LANES - 1)
    sels = tuple(_bf((lane2 == 2 * d + 1).astype(F32)) for d in range(2))

    def prep_group(q_ref, k_ref, v_ref, g_ref, c0, n_grp, n_seq, base):
        chunks = range(n_grp)
        units = [(u, d) for u in chunks for d in range(2)]
        rows = [pl.multiple_of((c0 + u) * c, c) for u in chunks]
        q = [q_ref[pl.ds(r, c), :] for r in rows]
        k = [k_ref[pl.ds(r, c), :] for r in rows]
        v = [v_ref[pl.ds(r, c), :] for r in rows]
        g = [g_ref[pl.ds(r, c), :] for r in rows]
        qb = [_bf(x) for x in q]
        kb = [_bf(x) for x in k]
        kk = _each(_dot_nt, kb, kb)
        qk = _each(_dot_nt, qb, kb)
        sig = [_sigmoid(x) for x in g]
        gdec = [neg_a * _softplus(x + dtb) for x in g]
        tot = [jnp.sum(x, axis=0, keepdims=True) for x in gdec]
        cum_f = [_dot(cums[0], jnp.concatenate(_split2(x), axis=1)) for x in gdec]
        cum_f = [x[:, :LANES] + x[:, LANES:] for x in cum_f]
        cum = [jnp.where(lane < 2, cf, t - cf + x) for cf, t, x in zip(cum_f, tot, gdec)]
        gcum = [cum[u][:, 2 * d + 1:2 * d + 2] for u, d in units]
        g_end = [tot[u][:, 2 * d + 1:2 * d + 2] for u, d in units]
        beta = [sig[u][:, 2 * d:2 * d + 1] for u, d in units]
        cum_hl = [jnp.concatenate(_split2(x), axis=1) for x in cum]
        g_row = [_dot_nt(sels[d], cum_hl[u]) for u, d in units]
        decay = [jnp.exp(jnp.where(incl[d], gc - gr, -jnp.inf)) for (u, d), gc, gr in zip(units, gcum, g_row)]
        a = [jnp.where(strict[d], bt * dc * kk[u], 0.0) for (u, d), bt, dc in zip(units, beta, decay)]
        tinv = _unit_tri_inverse(a, eye, ri, ci)
        gam = [jnp.exp(x) for x in gcum]
        rhs = [jnp.concatenate([bt * v[u], (bt * gm) * k[u]], axis=1) for (u, d), bt, gm in zip(units, beta, gam)]
        sol = [_bf(_dot(_bf(ti), _bf(r))) for ti, r in zip(tinv, rhs)]
        attn = [_bf(qk[u] * dc) for (u, d), dc in zip(units, decay)]
        kdec = [_bf(k[u] * jnp.exp(ge - gc)) for (u, d), ge, gc in zip(units, g_end, gcum)]
        qo = _each(_dot, attn, sol)
        pn = _each(_dot_tn, kdec, sol)
        for i, (u, d) in enumerate(units):
            chunk = c0 + u
            pos = base + (chunk if d == 0 else n_seq - 1 - chunk)
            p_ref[d, pos] = _bf(pn[i][:, LANES:])
            n_ref[d, pos] = pn[i][:, :LANES]
            qt_ref[d, pos] = _bf(gam[i] * q[u] - qo[i][:, LANES:])
            o_ref[d, pos] = qo[i][:, :LANES]
            eg_ref[d, pos] = jnp.broadcast_to(jnp.exp(g_end[i]), (8, LANES))

    grp_c = math.gcd(n_ctx, DN_GROUP)
    grp_l = math.gcd(n_lat, DN_GROUP)
    for grp in range(n_ctx // grp_c):
        prep_group(sqc_ref, skc_ref, svc_ref, gc_ref, grp * grp_c, grp_c, n_ctx, 0)

    def prep_body(grp, carry):
        prep_group(sql_ref, skl_ref, svl_ref, gl_ref, grp * grp_l, grp_l, n_lat, n_ctx)
        return carry

    lax.fori_loop(0, n_lat // grp_l, prep_body, 0)

    def scan_body(dst_refs, n_seq, base):
        def body(i, states):
            pos = base + i
            rows = (pl.multiple_of(i * c, c), pl.multiple_of((n_seq - 1 - i) * c, c))
            new = []
            for d in range(2):
                s = states[d]
                sb = _bf(s)
                dst_refs[d][pl.ds(rows[d], c), :] = _dot(qt_ref[d, pos], sb) + o_ref[d, pos]
                new.append(eg_ref[d, pos][0:1, :] * s - _dot(p_ref[d, pos], sb) + n_ref[d, pos])
            return tuple(new)
        return body

    zero = jnp.zeros((LANES, LANES), F32)
    states = lax.fori_loop(0, n_ctx, scan_body((yc_ref, obc_ref), n_ctx, 0), (zero, zero))
    lax.fori_loop(0, n_lat, scan_body((yl_ref, obl_ref), n_lat, n_ctx), states, unroll=4)

    ng = ng_ref[...]

    def finish(y_ref, ob_ref, z_ref):
        n_rows = y_ref.shape[0]
        tile = 256
        for r in range(0, n_rows, tile):
            sl = slice(r, r + tile)
            y_ref[sl, :] = _rms(y_ref[sl, :] + ob_ref[sl, :], ng) * _silu(z_ref[sl, :])

    finish(yc_ref, obc_ref, zc_ref)
    finish(yl_ref, obl_ref, zl_ref)


def _deltanet(pl_f32, pc_f32, conv_w, alog_rows, dtb_rows, ng):
    b, t, _ = pl_f32.shape
    tc = pc_f32.shape[1]
    h = DN_HEADS
    n_tot = (t + tc) // DN_CHUNK

    def specs(n):
        return [pl.BlockSpec((None, n, LANES), lambda bi, hi: (bi, 0, hi)),
                pl.BlockSpec((None, n, LANES), lambda bi, hi: (bi, 0, h + hi)),
                pl.BlockSpec((None, n, LANES), lambda bi, hi: (bi, 0, 2 * h + hi)),
                pl.BlockSpec((None, n, LANES), lambda bi, hi: (bi, 0, 3 * h + hi)),
                pl.BlockSpec((None, n, LANES), lambda bi, hi: (bi, 0, 4 * h + hi))]
    in_specs = specs(t) + specs(tc) + [
        pl.BlockSpec((3, LANES), lambda bi, hi: (0, hi)),
        pl.BlockSpec((3, LANES), lambda bi, hi: (0, h + hi)),
        pl.BlockSpec((3, LANES), lambda bi, hi: (0, 2 * h + hi)),
        pl.BlockSpec((None, 1, LANES), lambda bi, hi: (hi, 0, 0)),
        pl.BlockSpec((None, 1, LANES), lambda bi, hi: (hi, 0, 0)),
        pl.BlockSpec((1, LANES), lambda bi, hi: (0, 0))]
    yl, yc = pl.pallas_call(
        functools.partial(_dn_kernel, n_lat=t // DN_CHUNK, n_ctx=tc // DN_CHUNK),
        grid=(b, h),
        in_specs=in_specs,
        out_specs=[pl.BlockSpec((None, t, LANES), lambda bi, hi: (bi, 0, hi)),
                   pl.BlockSpec((None, tc, LANES), lambda bi, hi: (bi, 0, hi))],
        out_shape=[jax.ShapeDtypeStruct((b, t, h * LANES), F32),
                   jax.ShapeDtypeStruct((b, tc, h * LANES), F32)],
        scratch_shapes=[pltpu.VMEM((t, LANES), F32)] * 3 + [pltpu.VMEM((tc, LANES), F32)] * 3
                       + [pltpu.VMEM((t, LANES), F32), pltpu.VMEM((tc, LANES), F32),
                          pltpu.VMEM((2, n_tot, LANES, LANES), BF16), pltpu.VMEM((2, n_tot, LANES, LANES), F32),
                          pltpu.VMEM((2, n_tot, DN_CHUNK, LANES), BF16), pltpu.VMEM((2, n_tot, DN_CHUNK, LANES), F32),
                          pltpu.VMEM((2, n_tot, 8, LANES), F32)],
        compiler_params=_params("arbitrary", "arbitrary"),
        name="deltanet",
    )(*([pl_f32] * 5 + [pc_f32] * 5 + [conv_w, conv_w, conv_w, alog_rows, dtb_rows, ng.reshape(1, LANES)]))
    return yl, yc


_FF_CHUNKS = ((0, 1536), (1536, D_FF))


def _post_kernel(*refs, final):
    if final:
        x_ref, ya_ref, yb_ref, mod_ref, gf_ref, wa_ref, wb_ref, wg_ref, wu_ref, wd_ref, gfin_ref, o_ref = refs
    else:
        x_ref, ya_ref, yb_ref, mod_ref, gf_ref, wa_ref, wb_ref, wg_ref, wu_ref, wd_ref, o_ref = refs
    mod = mod_ref[...]
    ga1 = mod[:, 2 * D_MODEL:3 * D_MODEL]
    ga2 = mod[:, 5 * D_MODEL:6 * D_MODEL]
    ya = jnp.concatenate([_bf(ya_ref[g]) for g in range(ya_ref.shape[0])], axis=1)
    y = _dot(ya, wa_ref[...]) + _dot(_bf(yb_ref[...]), wb_ref[...])
    x1 = x_ref[...] + ga1 * y
    hb = _bf(_norm_mod(x1, gf_ref[...], mod, 3))
    acc = None
    for lo, hi in _FF_CHUNKS:
        gate = _dot(hb, wg_ref[:, lo:hi])
        up = _dot(hb, wu_ref[:, lo:hi])
        part = _dot(_bf(_silu(gate) * up), wd_ref[lo:hi, :])
        acc = part if acc is None else acc + part
    out = x1 + ga2 * acc
    if final:
        out = _rms(out, gfin_ref[...])
    o_ref[...] = out


def _post(x, ya, yb, mods, gf, w_out, oidx, wg, wu, wd, layer, gfin, *, mod_row, tm):
    b, t, _ = x.shape
    rows = b * t
    nb = rows // tm
    final = gfin is not None
    ga_groups = ya.shape[1]
    na, nbw = ga_groups * LANES, yb.shape[-1]
    bpb = t // tm

    def const(shape, slot, row_block=0):
        return pl.BlockSpec((None,) + shape, lambda i: (slot, row_block, 0), pipeline_mode=pl.Buffered(1))
    assert na == nbw
    in_specs = [pl.BlockSpec((tm, D_MODEL), lambda i: (i, 0)),
                pl.BlockSpec((None, ga_groups, tm, LANES), lambda i: (i // bpb, 0, i % bpb, 0)),
                pl.BlockSpec((tm, nbw), lambda i: (i, 0)),
                pl.BlockSpec((None, 1, 6 * D_MODEL), lambda i: (mod_row(i), 0, 0)),
                pl.BlockSpec((1, D_MODEL), lambda i: (0, 0)),
                const((na, D_MODEL), oidx, 0), const((nbw, D_MODEL), oidx, 1),
                const((D_MODEL, D_FF), layer), const((D_MODEL, D_FF), layer), const((D_FF, D_MODEL), layer)]
    args = [x.reshape(rows, D_MODEL), ya, yb.reshape(rows, nbw), mods,
            gf.reshape(1, D_MODEL), w_out, w_out, wg, wu, wd]
    if final:
        in_specs.append(pl.BlockSpec((1, D_MODEL), lambda i: (0, 0)))
        args.append(gfin.reshape(1, D_MODEL))
    out = pl.pallas_call(
        functools.partial(_post_kernel, final=final),
        grid=(nb,),
        in_specs=in_specs,
        out_specs=pl.BlockSpec((tm, D_MODEL), lambda i: (i, 0)),
        out_shape=jax.ShapeDtypeStruct((rows, D_MODEL), F32),
        compiler_params=_params("arbitrary"),
        name="post_ffn",
    )(*args)
    return out.reshape(b, t, D_MODEL)


def _rope_tables(n_tok):
    rows = n_tok // GRID_W
    quarter = HD // 4
    freqs = ROPE_THETA ** (-jnp.arange(quarter, dtype=F32) / quarter)
    row = jnp.repeat(jnp.arange(rows, dtype=F32), GRID_W)
    col = jnp.tile(jnp.arange(GRID_W, dtype=F32), rows)
    ang = jnp.concatenate([row[:, None] * freqs, col[:, None] * freqs], axis=-1)
    cos, sin = jnp.cos(ang), jnp.sin(ang)
    lane = np.arange(LANES)
    d = lane % HD
    src = (d // 32) * quarter + (d % quarter)
    first = (d % 32) < quarter
    c = cos[:, src]
    s = sin[:, src]
    s1 = jnp.where(first[None, :], -s, 0.0)
    s2 = jnp.where(first[None, :], 0.0, s)
    return c, s1, s2


def _pad_lanes(a, lo, width=LANES):
    pads = [(0, 0)] * (a.ndim - 1) + [(lo, width - lo - a.shape[-1])]
    return jnp.pad(a, pads)


def _even_weights(w):
    lead = w.shape[:-1]
    g = w[..., 3072:3088].reshape(lead + (4, 2, 2))
    g = jnp.moveaxis(g, -2, -3).reshape(lead + (2, 8))
    return jnp.concatenate([w[..., :3072], _pad_lanes(g, 0).reshape(lead + (2 * LANES,))], axis=-1)


def _odd_weights(w):
    lead = w.shape[:-1]
    per_kv = GQ_HEADS // GQ_KV
    cq = w[..., 0:GQ_HEADS * HD].reshape(lead + (GQ_KV, per_kv, HD))
    q = jnp.concatenate([_pad_lanes(cq[..., j, :, :], j * HD) for j in range(GQ_KV)], axis=-2)
    ck = w[..., 512:640].reshape(lead + (GQ_KV, HD))
    k = jnp.concatenate([_pad_lanes(ck[..., j, :], j * HD) for j in range(GQ_KV)], axis=-1)
    g = w[..., 2816:2832].reshape(lead + (2, 2, DN_HEADS))
    g = jnp.moveaxis(g, -1, -3).reshape(lead + (DN_HEADS, 4))
    return jnp.concatenate([q.reshape(lead + (GQ_HEADS * LANES,)), k, w[..., 640:2816],
                            _pad_lanes(g, 0).reshape(lead + (DN_HEADS * LANES,))], axis=-1)


def kernel(x, c, ctx, c_ctx, w_ada, b_ada, g_mix, g_ffn, w_in_e, b_gate_e, da_lam, da_norm_g, ml_norm_g,
           w_out_e, w_in_o, qk_norm_g, dn_conv, dn_a_log, dn_dt_bias, dn_norm_g, w_out_o, w_gate, w_up,
           w_down, g_final):
    b, t, _ = x.shape
    tc = ctx.shape[1]
    tables = _rope_tables(t)

    cv = jnp.zeros((16, D_MODEL), F32).at[:b].set(c).at[b].set(c_ctx)
    mods = _ada_mods(cv, w_ada, b_ada).reshape(DEPTH, 16, 1, 6 * D_MODEL)

    w_e = _even_weights(_bf(w_in_e))
    w_o = _odd_weights(_bf(w_in_o))
    wout_e, wout_o = _bf(w_out_e), _bf(w_out_o)
    wg, wu, wd = _bf(w_gate), _bf(w_up), _bf(w_down)
    tm_l = 512
    tm_c = 256
    bpb = t // tm_l
    lat_row = lambda i: i // bpb
    ctx_row = lambda i: b

    xl, xc = x, ctx
    for layer in range(DEPTH):
        emit_ctx = layer < DEPTH - 1
        m = mods[layer]
        if layer % 2 == 0:
            e = layer // 2
            lam_init = 0.8 - 0.6 * math.exp(-0.3 * layer)
            pl_bf, pl_f = _inproj(xl, m, g_mix[layer], w_e, e, [], tables, even=True, seq_len=t, mod_row=lat_row, tm=tm_l)
            pc_bf, pc_f = _inproj(xc, m, g_mix[layer], w_e, e, [], None, even=True, seq_len=tc, mod_row=ctx_row, tm=tm_c)
            ya_l = _diff_attn(pl_bf, [pc_bf, pl_bf], da_lam[e], da_norm_g[e], lam_init, 512, True)
            gbias = jnp.zeros((2, 1, LANES), F32)
            gb = b_gate_e[e].reshape(4, 2, 2)
            gbias = gbias.at[:, 0, :8].set(gb.transpose(1, 0, 2).reshape(2, 8))
            yb_l, yb_c = _mlstm(pl_f, pc_f, gbias, ml_norm_g[e])
            if emit_ctx:
                ya_c = _diff_attn(pc_bf, [pc_bf], da_lam[e], da_norm_g[e], lam_init, tc, False)
            w_out, oidx = wout_e, e
        else:
            o = layer // 2
            qkg = jnp.zeros((2 * GQ_KV, LANES), F32)
            for j in range(GQ_KV):
                qkg = qkg.at[j, j * HD:(j + 1) * HD].set(qk_norm_g[o, 0])
                qkg = qkg.at[GQ_KV + j, j * HD:(j + 1) * HD].set(qk_norm_g[o, 1])
            pl_bf, pl_f = _inproj(xl, m, g_mix[layer], w_o, o, [qkg], tables, even=False, seq_len=t, mod_row=lat_row, tm=tm_l)
            pc_bf, pc_f = _inproj(xc, m, g_mix[layer], w_o, o, [qkg], None, even=False, seq_len=tc, mod_row=ctx_row, tm=tm_c)
            ya_l = _gqa_attn(pl_bf, [pc_bf, pl_bf], 512, True)
            alog = jnp.zeros((DN_HEADS, 1, LANES), F32)
            dtb = jnp.zeros((DN_HEADS, 1, LANES), F32)
            for d in range(2):
                alog = alog.at[:, 0, 2 * d + 1].set(dn_a_log[o, d])
                dtb = dtb.at[:, 0, 2 * d + 1].set(dn_dt_bias[o, d])
            yb_l, yb_c = _deltanet(pl_f, pc_f, dn_conv[o], alog, dtb, dn_norm_g[o])
            if emit_ctx:
                ya_c = _gqa_attn(pc_bf, [pc_bf], tc, False)
            w_out, oidx = wout_o, o
        gfin = g_final if layer == DEPTH - 1 else None
        xl = _post(xl, ya_l, yb_l, m, g_ffn[layer], w_out, oidx, wg, wu, wd, layer, gfin, mod_row=lat_row, tm=tm_l)
        if emit_ctx:
            xc = _post(xc, ya_c, yb_c, m, g_ffn[layer], w_out, oidx, wg, wu, wd, layer, None, mod_row=ctx_row, tm=tm_c)
    return xl
```
